```python
import math
import jax, jax.numpy as jnp
from jax import lax
import numpy as np


D_MODEL = 1024
BATCH = 16
SEQ = 2048
DEPTH = 2
DEC_BATCH = 128
DEC_SEQ = 1
PAST_LEN = 8192
PAGE_SIZE = 128

HEAD_DIM = 64
ATTN_SCALE = HEAD_DIM ** -0.5
A_HEADS = 8
A_WIDTH = A_HEADS * HEAD_DIM
B_HEADS = 8
B_NOPE = 64
B_ROPE = 32
B_VDIM = 64
B_Q_RANK = 384
B_KV_RANK = 256
B_WIDTH = B_HEADS * B_VDIM
B_LATENT = B_KV_RANK + B_ROPE
MLA_SCALE = (B_NOPE + B_ROPE) ** -0.5
ROPE_THETA = 10000.0
C_WINDOWS = (128, 512, 2048)
C_DILATIONS = (1, 4, 16)
C_GROUPS = 3
C_HEADS = 8
C_WIDTH = C_HEADS * HEAD_DIM
C_STEPS = 128
BLOCK = 128
NORM_EPS = 1e-5
DN_ALPHA = (2 * DEPTH) ** 0.25
DN_BETA = (8 * DEPTH) ** -0.25
IN0_SIZES = (A_WIDTH, A_WIDTH, A_WIDTH, B_Q_RANK, B_KV_RANK, B_ROPE, A_WIDTH + B_WIDTH)
IN1_WIDTH = C_GROUPS * 3 * C_WIDTH + C_WIDTH

kernel_name = 'hybrid_stickbreak_mla_dilated_decoder'


def layer_norm(x, g, b):
    xf = x.astype(jnp.float32)
    mu = jnp.mean(xf, -1, keepdims=True)
    var = jnp.mean(jnp.square(xf - mu), -1, keepdims=True)
    return ((xf - mu) * lax.rsqrt(var + NORM_EPS) * g.astype(jnp.float32) + b.astype(jnp.float32)).astype(x.dtype)


def rms_norm(x, g):
    xf = x.astype(jnp.float32)
    return (xf * lax.rsqrt(jnp.mean(jnp.square(xf), -1, keepdims=True) + NORM_EPS) * g.astype(jnp.float32)).astype(x.dtype)


def rope(x, pos):
    inv = ROPE_THETA ** (-jnp.arange(0, B_ROPE, 2, dtype=jnp.float32) / B_ROPE)
    ang = pos.astype(jnp.float32)[:, None] * inv[None, :]
    ang = ang.reshape(ang.shape[0], *([1] * (x.ndim - 3)), ang.shape[1])
    cos, sin = jnp.cos(ang), jnp.sin(ang)
    x1, x2 = jnp.split(x.astype(jnp.float32), 2, axis=-1)
    return jnp.concatenate([x1 * cos - x2 * sin, x1 * sin + x2 * cos], -1).astype(x.dtype)


def alibi_slopes():
    n = C_GROUPS * C_HEADS
    return (2.0 ** (-8.0 * jnp.arange(1, n + 1, dtype=jnp.float32) / n)).reshape(C_GROUPS, C_HEADS)


def sweep_query_blocks(fn, q_arrays, q_pos):
    b, t = q_arrays[0].shape[:2]
    nb = t // BLOCK
    blocked = tuple(jnp.swapaxes(a.reshape(b, nb, BLOCK, *a.shape[2:]), 0, 1) for a in q_arrays)
    out = lax.map(lambda xs: fn(*xs), blocked + (q_pos.reshape(nb, BLOCK),))
    return jnp.swapaxes(out, 0, 1).reshape(b, t, *out.shape[3:])


def stick_breaking(q, k, v, q_pos, k_pos):
    z = jnp.einsum('bqhd,bkhd->bhqk', q, k).astype(jnp.float32) * ATTN_SCALE
    earlier = k_pos[None, :] < q_pos[:, None]
    log_keep = jnp.where(earlier, jax.nn.log_sigmoid(-z), 0.0)
    log_tail = lax.cumsum(log_keep, axis=3, reverse=True) - log_keep
    w = jnp.where(earlier, jnp.exp(jax.nn.log_sigmoid(z) + log_tail), 0.0)
    return jnp.einsum('bhqk,bkhd->bqhd', w.astype(v.dtype), v)


def mla_attend(q_lat, q_rope, latent, q_pos, k_pos):
    c_kv, k_rope = latent[..., :B_KV_RANK], latent[..., B_KV_RANK:]
    s = (jnp.einsum('bqhc,bkc->bhqk', q_lat, c_kv)
         + jnp.einsum('bqhr,bkr->bhqk', q_rope, k_rope)).astype(jnp.float32) * MLA_SCALE
    s = jnp.where(k_pos[None, :] <= q_pos[:, None], s, -jnp.inf)
    p = jax.nn.softmax(s, axis=-1).astype(latent.dtype)
    return jnp.einsum('bhqk,bkc->bqhc', p, c_kv)


def layer0_project(x, pos, w_in0, g_cq, g_ckv, w_uq, w_uk):
    b, t, _ = x.shape
    cuts = np.cumsum(IN0_SIZES)[:-1].tolist()
    qa, ka, va, dq, dkv, kr, gate = jnp.split(x @ w_in0, cuts, axis=-1)
    qa = qa.reshape(b, t, A_HEADS, HEAD_DIM)
    kv_a = jnp.stack([ka, va], axis=2).reshape(b, t, 2, A_HEADS, HEAD_DIM)
    qb = (rms_norm(dq, g_cq) @ w_uq).reshape(b, t, B_HEADS, B_NOPE + B_ROPE)
    q_lat = jnp.einsum('bthn,chn->bthc', qb[..., :B_NOPE], w_uk.reshape(B_KV_RANK, B_HEADS, B_NOPE))
    q_rope = rope(qb[..., B_NOPE:], pos)
    latent = jnp.concatenate([rms_norm(dkv, g_ckv), rope(kr, pos)], axis=-1)
    return qa, kv_a, q_lat, q_rope, latent, gate


def layer0_finish(x, o_a, o_lat, gate, w_uv, w_out0, ln_g, ln_b):
    b, t, _ = x.shape
    o_b = jnp.einsum('bthc,chv->bthv', o_lat, w_uv.reshape(B_KV_RANK, B_HEADS, B_VDIM))
    o = jnp.concatenate([o_a.reshape(b, t, A_WIDTH), o_b.reshape(b, t, B_WIDTH)], -1) * jax.nn.silu(gate)
    return layer_norm(DN_ALPHA * x + o @ w_out0, ln_g, ln_b)


def dilated_window_prompt(q, k, v, r, slopes):
    b, t, h, d = q.shape
    ls = t // r
    nb = -(-ls // BLOCK)
    lp = nb * BLOCK

    def to_blocks(a):
        a = jnp.swapaxes(a.reshape(b, ls, r, h, d), 1, 2)
        a = jnp.pad(a, ((0, 0), (0, 0), (0, lp - ls), (0, 0), (0, 0)))
        return a.reshape(b, r, nb, BLOCK, h, d)

    def to_band(a):
        prev = jnp.pad(a, ((0, 0), (0, 0), (1, 0), (0, 0), (0, 0), (0, 0)))[:, :, :-1]
        return jnp.concatenate([prev, a], axis=3)

    qb = to_blocks(q)
    kb, vb = to_band(to_blocks(k)), to_band(to_blocks(v))
    qq = jnp.arange(BLOCK)[:, None]
    kk = jnp.arange(2 * BLOCK)[None, :]
    steps = BLOCK + qq - kk
    key_idx = (jnp.arange(nb)[:, None, None] - 1) * BLOCK + kk[None]
    valid = (steps >= 0) & (steps <= C_STEPS) & (key_idx >= 0)
    s = jnp.einsum('bsnqhd,bsnkhd->bsnhqk', qb, kb).astype(jnp.float32) * ATTN_SCALE
    s = s - slopes[:, None, None] * (r * steps).astype(jnp.float32)
    s = jnp.where(valid[:, None], s, -jnp.inf)
    lse = jax.nn.logsumexp(s, axis=-1)
    p = jnp.exp(s - lse[..., None]).astype(v.dtype)
    o = jnp.einsum('bsnhqk,bsnkhd->bsnqhd', p, vb)
    o = jnp.swapaxes(o.reshape(b, r, lp, h, d)[:, :, :ls], 1, 2).reshape(b, t, h, d)
    lse = jnp.swapaxes(jnp.swapaxes(lse, 3, 4).reshape(b, r, lp, h)[:, :, :ls], 1, 2).reshape(b, t, h)
    return o, lse


def dilated_window_sample(q, kv_all, r, slopes):
    t = q.shape[1]
    wb = kv_all.shape[1] - t
    steps = jnp.arange(C_STEPS + 1)
    idx = wb + jnp.arange(t)[:, None] - r * steps[None, :]
    valid = idx >= 0
    kvg = kv_all[:, jnp.maximum(idx, 0)]
    s = jnp.einsum('bthd,btmhd->bthm', q, kvg[:, :, :, 0]).astype(jnp.float32) * ATTN_SCALE
    s = s - slopes[:, None] * (r * steps).astype(jnp.float32)
    s = jnp.where(valid[:, None, :], s, -jnp.inf)
    lse = jax.nn.logsumexp(s, axis=-1)
    p = jnp.exp(s - lse[..., None]).astype(kv_all.dtype)
    return jnp.einsum('bthm,btmhd->bthd', p, kvg[:, :, :, 1]), lse


def combine_groups(outs, lses):
    wts = jax.nn.softmax(jnp.stack(lses), axis=0)
    return jnp.sum(wts[..., None] * jnp.stack(outs).astype(jnp.float32), axis=0)


def layer1_project(x, w_in1):
    b, t, _ = x.shape
    h = x @ w_in1
    qkv = h[..., :C_GROUPS * 3 * C_WIDTH].reshape(b, t, C_GROUPS, 3, C_HEADS, HEAD_DIM)
    return qkv, h[..., C_GROUPS * 3 * C_WIDTH:]


def layer1_finish(x, o, gate, w_out1, ln_g, ln_b):
    b, t, _ = x.shape
    o = o.reshape(b, t, C_WIDTH).astype(gate.dtype) * jax.nn.silu(gate)
    return layer_norm(DN_ALPHA * x + o @ w_out1, ln_g, ln_b)


def setup_inputs(seed: int = 0) -> dict:
    key = jax.random.key(seed)
    ks = jax.random.split(key, 22)
    n_pages = PAST_LEN // PAGE_SIZE
    n_used = DEC_BATCH * n_pages
    n_phys = n_used + n_used // 4
    page_table = jax.random.permutation(ks[0], n_phys)[:n_used].reshape(DEC_BATCH, n_pages).astype(jnp.int32)

    def normal(k, shape, scale=1.0):
        return jax.random.normal(k, shape, jnp.float32) * scale

    def gain(k, n):
        return 1.0 + normal(k, (n,), 0.02)

    return {
        'x_prompt': normal(ks[1], (BATCH, SEQ, D_MODEL)),
        'x_sample': normal(ks[2], (DEC_BATCH, DEC_SEQ, D_MODEL)),
        'cache_a_kv': normal(ks[3], (n_phys, PAGE_SIZE, 2, A_HEADS, HEAD_DIM)),
        'cache_b_latent': normal(ks[4], (n_phys, PAGE_SIZE, B_LATENT)),
        'cache_c0_kv': normal(ks[5], (DEC_BATCH, min(C_WINDOWS[0], PAST_LEN), 2, C_HEADS, HEAD_DIM)),
        'cache_c1_kv': normal(ks[6], (DEC_BATCH, min(C_WINDOWS[1], PAST_LEN), 2, C_HEADS, HEAD_DIM)),
        'cache_c2_kv': normal(ks[7], (DEC_BATCH, min(C_WINDOWS[2], PAST_LEN), 2, C_HEADS, HEAD_DIM)),
        'page_table': page_table,
        'w_in0': normal(ks[8], (D_MODEL, sum(IN0_SIZES)), D_MODEL ** -0.5),
        'g_cq': gain(ks[9], B_Q_RANK),
        'g_ckv': gain(ks[10], B_KV_RANK),
        'w_uq': normal(ks[11], (B_Q_RANK, B_HEADS * (B_NOPE + B_ROPE)), B_Q_RANK ** -0.5),
        'w_uk': normal(ks[12], (B_KV_RANK, B_HEADS * B_NOPE), B_KV_RANK ** -0.5),
        'w_uv': normal(ks[13], (B_KV_RANK, B_WIDTH), B_KV_RANK ** -0.5),
        'w_out0': normal(ks[14], (A_WIDTH + B_WIDTH, D_MODEL), DN_BETA * (A_WIDTH + B_WIDTH) ** -0.5),
        'ln0_g': gain(ks[15], D_MODEL),
        'ln0_b': normal(ks[16], (D_MODEL,), 0.02),
        'w_in1': normal(ks[17], (D_MODEL, IN1_WIDTH), D_MODEL ** -0.5),
        'w_out1': normal(ks[18], (C_WIDTH, D_MODEL), DN_BETA * C_WIDTH ** -0.5),
        'ln1_g': gain(ks[19], D_MODEL),
        'ln1_b': normal(ks[20], (D_MODEL,), 0.02),
    }


def reference(x_prompt, x_sample, cache_a_kv, cache_b_latent, cache_c0_kv, cache_c1_kv, cache_c2_kv,
              page_table, w_in0, g_cq, g_ckv, w_uq, w_uk, w_uv, w_out0, ln0_g, ln0_b,
              w_in1, w_out1, ln1_g, ln1_b):
    db, n_pages = page_table.shape
    past_len = n_pages * cache_a_kv.shape[1]
    t_p, t_s = x_prompt.shape[1], x_sample.shape[1]
    pos_p = jnp.arange(t_p)
    pos_s = past_len + jnp.arange(t_s)
    k_pos_s = jnp.arange(past_len + t_s)
    c_caches = (cache_c0_kv, cache_c1_kv, cache_c2_kv)
    slopes = alibi_slopes()
    h_p, h_s = x_prompt, x_sample
    for layer in range(DEPTH):
        if layer % 2 == 0:
            qa, kv_a_p, q_lat, q_rope, lat_b_p, gate = layer0_project(h_p, pos_p, w_in0, g_cq, g_ckv, w_uq, w_uk)
            k_p, v_p = kv_a_p[:, :, 0], kv_a_p[:, :, 1]
            o_a = sweep_query_blocks(lambda q, qp: stick_breaking(q, k_p, v_p, qp, pos_p), (qa,), pos_p)
            o_lat = sweep_query_blocks(lambda ql, qr, qp: mla_attend(ql, qr, lat_b_p, qp, pos_p),
                                       (q_lat, q_rope), pos_p)
            h_p = layer0_finish(h_p, o_a, o_lat, gate, w_uv, w_out0, ln0_g, ln0_b)
            qa, kv_a_s, q_lat, q_rope, lat_b_s, gate = layer0_project(h_s, pos_s, w_in0, g_cq, g_ckv, w_uq, w_uk)
            k_all = jnp.concatenate([cache_a_kv[page_table, :, 0].reshape(db, past_len, A_HEADS, HEAD_DIM),
                                     kv_a_s[:, :, 0]], axis=1)
            v_all = jnp.concatenate([cache_a_kv[page_table, :, 1].reshape(db, past_len, A_HEADS, HEAD_DIM),
                                     kv_a_s[:, :, 1]], axis=1)
            o_a = stick_breaking(qa, k_all, v_all, pos_s, k_pos_s)
            lat_all = jnp.concatenate([cache_b_latent[page_table].reshape(db, past_len, B_LATENT), lat_b_s], axis=1)
            o_lat = mla_attend(q_lat, q_rope, lat_all, pos_s, k_pos_s)
            h_s = layer0_finish(h_s, o_a, o_lat, gate, w_uv, w_out0, ln0_g, ln0_b)
        else:
            qkv, gate = layer1_project(h_p, w_in1)
            outs, lses, c_new_p = [], [], []
            for g in range(C_GROUPS):
                kv = qkv[:, :, g, 1:]
                o, lse = dilated_window_prompt(qkv[:, :, g, 0], kv[:, :, 0], kv[:, :, 1], C_DILATIONS[g], slopes[g])
                outs.append(o)
                lses.append(lse)
                c_new_p.append(kv[:, t_p - min(C_WINDOWS[g], t_p):])
            h_p = layer1_finish(h_p, combine_groups(outs, lses), gate, w_out1, ln1_g, ln1_b)
            qkv, gate = layer1_project(h_s, w_in1)
            outs, lses, c_new_s = [], [], []
            for g in range(C_GROUPS):
                kv = qkv[:, :, g, 1:]
                kv_all = jnp.concatenate([c_caches[g], kv], axis=1)
                o, lse = dilated_window_sample(qkv[:, :, g, 0], kv_all, C_DILATIONS[g], slopes[g])
                outs.append(o)
                lses.append(lse)
                c_new_s.append(kv)
            h_s = layer1_finish(h_s, combine_groups(outs, lses), gate, w_out1, ln1_g, ln1_b)
    return (h_p, h_s, kv_a_p, kv_a_s, lat_b_p, lat_b_s,
            c_new_p[0], c_new_s[0], c_new_p[1], c_new_s[1], c_new_p[2], c_new_s[2])
```

```python
import functools

import numpy as np
import jax
import jax.numpy as jnp
from jax import lax
from jax.experimental import pallas as pl
from jax.experimental.pallas import tpu as pltpu

D_MODEL = 1024
DEPTH = 2
HEAD_DIM = 64
N_HEADS = 8
WIDTH = N_HEADS * HEAD_DIM
ATTN_SCALE = HEAD_DIM ** -0.5
B_NOPE = 64
B_ROPE = 32
B_HALF = B_ROPE // 2
B_Q_RANK = 384
B_KV_RANK = 256
B_LATENT = B_KV_RANK + B_ROPE
MLA_SCALE = (B_NOPE + B_ROPE) ** -0.5
ROPE_THETA = 10000.0
C_DILATIONS = (1, 4, 16)
C_GROUPS = 3
C_STEPS = 128
NORM_EPS = 1e-5
DN_ALPHA = (2 * DEPTH) ** 0.25
IN0_SIZES = (WIDTH, WIDTH, WIDTH, B_Q_RANK, B_KV_RANK, B_ROPE, 2 * WIDTH)

LANES = 128
PAIR = 2 * HEAD_DIM
N_PAIRS = N_HEADS // 2
V7X_VMEM_BYTES = 64 * 1024 * 1024
MXU_TILE = 256
DEC_ROWS = 16

NEG_INF = float("-inf")
F32 = jnp.float32
BF16 = jnp.bfloat16


def _cparams(semantics, vmem_bytes):
    return pltpu.CompilerParams(dimension_semantics=semantics,
                                vmem_limit_bytes=min(int(vmem_bytes), V7X_VMEM_BYTES - (4 << 20)))


def _dot(a, b):
    return jnp.dot(a, b, preferred_element_type=F32)


def _dot_t(a, b):
    return lax.dot_general(a, b, (((1,), (1,)), ((), ())), preferred_element_type=F32)


def _iota(shape, dim):
    return lax.broadcasted_iota(jnp.int32, shape, dim)


def _rms(x, g):
    return x * lax.rsqrt(jnp.mean(x * x, axis=-1, keepdims=True) + NORM_EPS) * g


def _softplus(z):
    return jnp.maximum(z, 0.0) + jnp.log(1.0 + jnp.exp(-jnp.abs(z)))


def _split_bf16(x):
    hi = x.astype(BF16)
    lo = (x - hi.astype(F32)).astype(BF16)
    return hi, lo


def _pair_half_mask(rows, head):
    lane = _iota((rows, PAIR), 1)
    return (lane >= HEAD_DIM) if head % 2 else (lane < HEAD_DIM)


def _keep(mask, x):
    return jnp.where(mask, x.astype(F32), 0.0).astype(x.dtype)


def _head_of_lane(shape, dim):
    return lax.shift_right_logical(_iota(shape, dim), 6)


_O_QA, _O_KV, _O_GATE = 0, WIDTH, 3 * WIDTH
_O_DQ = _O_GATE + 2 * WIDTH
_O_DKV = _O_DQ + B_Q_RANK
_O_KR = _O_DKV + B_KV_RANK
_N_IN0 = _O_KR + 2 * B_ROPE
_N_UQ = WIDTH + 4 * LANES
_T_COSQ, _T_SINQ, _T_COSK, _T_SINK, _N_TAB = 0, LANES, 2 * LANES, 3 * LANES, 4 * LANES


def _l0_proj_kernel(x_ref, tab_ref, w_ref, gcq_ref, gckv_ref, wuq_ref, wuk_ref, wuv_ref, e_ref,
                    qa_ref, kvf_ref, kvb_ref, gate_ref, lat_ref, qn_ref, qr_ref, kcat_ref, vb_ref):
    xb = x_ref[...].astype(BF16)
    qa_ref[...] = (_dot(xb, w_ref[:, _O_QA:_O_KV]) * ATTN_SCALE).astype(BF16)
    kv = _dot(xb, w_ref[:, _O_KV:_O_GATE])
    kvf_ref[...] = kv
    kvb_ref[...] = kv.astype(BF16)
    gate_ref[...] = _dot(xb, w_ref[:, _O_GATE:_O_DQ])
    dq = _dot(xb, w_ref[:, _O_DQ:_O_DKV])
    qf = _dot(_rms(dq, gcq_ref[...]).astype(BF16), wuq_ref[...])
    qn_ref[...] = qf[:, :WIDTH].astype(BF16)
    cosq = tab_ref[:, _T_COSQ:_T_COSQ + LANES]
    sinq = tab_ref[:, _T_SINQ:_T_SINQ + LANES]
    for s in range(2):
        nat = qf[:, WIDTH + s * LANES:WIDTH + (s + 1) * LANES]
        swp = qf[:, WIDTH + (2 + s) * LANES:WIDTH + (3 + s) * LANES]
        qr_ref[:, s * LANES:(s + 1) * LANES] = (nat * cosq + swp * sinq).astype(BF16)
    ckv = _rms(_dot(xb, w_ref[:, _O_DKV:_O_KR]), gckv_ref[...])
    lat_ref[:, :B_KV_RANK] = ckv
    kr2 = _dot(xb, w_ref[:, _O_KR:_N_IN0])
    krot = (kr2[:, :B_ROPE] * tab_ref[:, _T_COSK:_T_COSK + B_ROPE]
            + kr2[:, B_ROPE:] * tab_ref[:, _T_SINK:_T_SINK + B_ROPE])
    lat_ref[:, B_KV_RANK:] = krot
    ckv_b = ckv.astype(BF16)
    kn = _dot(ckv_b, wuk_ref[...]).astype(BF16)
    vb_ref[...] = _dot(ckv_b, wuv_ref[...]).astype(BF16)
    kslab = _dot(krot.astype(BF16), e_ref[...]).astype(BF16)
    for p in range(N_PAIRS):
        kcat_ref[:, p * MXU_TILE:p * MXU_TILE + PAIR] = kn[:, p * PAIR:(p + 1) * PAIR]
        kcat_ref[:, p * MXU_TILE + PAIR:(p + 1) * MXU_TILE] = kslab


def _l0_proj(x2, tab, w0, gcq, gckv, wuq, wuk, wuv, emat, tm):
    m = x2.shape[0]
    n_tab_blocks = tab.shape[0] // tm
    row = lambda n: pl.BlockSpec((tm, n), lambda i: (i, 0))
    full = lambda a: pl.BlockSpec(a.shape, lambda i: (0,) * a.ndim, pipeline_mode=pl.Buffered(1))
    outs = [(WIDTH, BF16), (2 * WIDTH, F32), (2 * WIDTH, BF16), (2 * WIDTH, F32), (B_LATENT, F32),
            (WIDTH, BF16), (2 * LANES, BF16), (N_PAIRS * MXU_TILE, BF16), (WIDTH, BF16)]
    return pl.pallas_call(
        _l0_proj_kernel,
        grid=(m // tm,),
        in_specs=[row(D_MODEL), pl.BlockSpec((tm, _N_TAB), lambda i: (i % n_tab_blocks, 0)),
                  full(w0), full(gcq), full(gckv), full(wuq), full(wuk), full(wuv), full(emat)],
        out_specs=[row(n) for n, _ in outs],
        out_shape=[jax.ShapeDtypeStruct((m, n), dt) for n, dt in outs],
        compiler_params=_cparams(("parallel",), 48 << 20),
        name="l0_proj",
    )(x2, tab, w0, gcq, gckv, wuq, wuk, wuv, emat)


def _sb_attn_kernel(q_ref, kv_ref, u_ref, o_ref, qp_ref, acc_ref, carry_ref, *, tq):
    i = pl.program_id(1)
    acc_ref[...] = jnp.zeros_like(acc_ref)
    carry_ref[...] = jnp.zeros_like(carry_ref)
    for h in range(N_HEADS):
        qp = q_ref[:, (h // 2) * PAIR:(h // 2 + 1) * PAIR]
        qp_ref[h] = _keep(_pair_half_mask(tq, h), qp)

    def block(j, masked):
        start = pl.multiple_of(j * tq, tq)
        u = u_ref[...]
        if masked:
            strict = _iota((tq, tq), 1) < _iota((tq, tq), 0)
        for h in range(N_HEADS):
            p = h // 2
            kb = kv_ref[pl.ds(start, tq), p * PAIR:(p + 1) * PAIR]
            vb = kv_ref[pl.ds(start, tq), WIDTH + p * PAIR:WIDTH + (p + 1) * PAIR]
            z = _dot_t(qp_ref[h], kb)
            sp = _softplus(z)
            lk = jnp.where(strict, -sp, 0.0) if masked else -sp
            hi, lo = _split_bf16(lk)
            tail = _dot(hi, u) + _dot(lo, u)
            w = jnp.exp(z - sp + tail + carry_ref[h])
            if masked:
                w = jnp.where(strict, w, 0.0)
            acc_ref[h] += _dot(w.astype(BF16), vb)
            carry_ref[h] += jnp.sum(lk, axis=1, keepdims=True)

    block(i, True)

    def body(jj, c):
        block(i - jj, False)
        return c

    lax.fori_loop(1, i + 1, body, 0)
    for p in range(N_PAIRS):
        o_ref[:, p * PAIR:(p + 1) * PAIR] = jnp.where(_pair_half_mask(tq, 0), acc_ref[2 * p], acc_ref[2 * p + 1])


def _sb_attn(qa, kvb, umat, tq):
    b, t, _ = qa.shape
    return pl.pallas_call(
        functools.partial(_sb_attn_kernel, tq=tq),
        grid=(b, t // tq),
        in_specs=[pl.BlockSpec((None, tq, WIDTH), lambda bi, i: (bi, i, 0)),
                  pl.BlockSpec((None, t, 2 * WIDTH), lambda bi, i: (bi, 0, 0)),
                  pl.BlockSpec(umat.shape, lambda bi, i: (0, 0))],
        out_specs=pl.BlockSpec((None, tq, WIDTH), lambda bi, i: (bi, i, 0)),
        out_shape=jax.ShapeDtypeStruct((b, t, WIDTH), F32),
        scratch_shapes=[pltpu.VMEM((N_HEADS, tq, PAIR), BF16), pltpu.VMEM((N_HEADS, tq, PAIR), F32),
                        pltpu.VMEM((N_HEADS, tq, 1), F32)],
        compiler_params=_cparams(("parallel", "arbitrary"), 40 << 20),
        name="sb_attn",
    )(qa, kvb, umat)


def _rope_slab_mask(rows, head):
    lane = _iota((rows, LANES), 1)
    return (lax.shift_right_logical(lane, 4) & 3) == (head % 4)


def _mla_attn_kernel(qn_ref, qr_ref, kcat_ref, vb_ref, o_ref, qc_ref, acc_ref, m_ref, l_ref, *, tq):
    i = pl.program_id(1)
    acc_ref[...] = jnp.zeros_like(acc_ref)
    m_ref[...] = jnp.full_like(m_ref, NEG_INF)
    l_ref[...] = jnp.zeros_like(l_ref)
    for h in range(N_HEADS):
        qp = qn_ref[:, (h // 2) * PAIR:(h // 2 + 1) * PAIR]
        qs = qr_ref[:, (h // 4) * LANES:(h // 4 + 1) * LANES]
        qc_ref[h, :, :PAIR] = _keep(_pair_half_mask(tq, h), qp)
        qc_ref[h, :, PAIR:] = _keep(_rope_slab_mask(tq, h), qs)

    def block(j, masked):
        start = pl.multiple_of(j * tq, tq)
        if masked:
            causal = _iota((tq, tq), 1) <= _iota((tq, tq), 0)
        for h in range(N_HEADS):
            p = h // 2
            kb = kcat_ref[pl.ds(start, tq), p * MXU_TILE:(p + 1) * MXU_TILE]
            vb = vb_ref[pl.ds(start, tq), p * PAIR:(p + 1) * PAIR]
            s = _dot_t(qc_ref[h], kb) * MLA_SCALE
            if masked:
                s = jnp.where(causal, s, NEG_INF)
            m_old = m_ref[h]
            m_new = jnp.maximum(m_old, jnp.max(s, axis=1, keepdims=True))
            pr = jnp.exp(s - m_new)
            alpha = jnp.exp(m_old - m_new)
            l_ref[h] = alpha * l_ref[h] + jnp.sum(pr, axis=1, keepdims=True)
            acc_ref[h] = alpha * acc_ref[h] + _dot(pr.astype(BF16), vb)
            m_ref[h] = m_new

    block(i, True)

    def body(j, c):
        block(j, False)
        return c

    lax.fori_loop(0, i, body, 0)
    for p in range(N_PAIRS):
        o_ref[:, p * PAIR:(p + 1) * PAIR] = jnp.where(
            _pair_half_mask(tq, 0), acc_ref[2 * p] / l_ref[2 * p], acc_ref[2 * p + 1] / l_ref[2 * p + 1])


def _mla_attn(qn, qr, kcat, vb, tq):
    b, t, _ = qn.shape
    return pl.pallas_call(
        functools.partial(_mla_attn_kernel, tq=tq),
        grid=(b, t // tq),
        in_specs=[pl.BlockSpec((None, tq, WIDTH), lambda bi, i: (bi, i, 0)),
                  pl.BlockSpec((None, tq, 2 * LANES), lambda bi, i: (bi, i, 0)),
                  pl.BlockSpec((None, t, N_PAIRS * MXU_TILE), lambda bi, i: (bi, 0, 0)),
                  pl.BlockSpec((None, t, WIDTH), lambda bi, i: (bi, 0, 0))],
        out_specs=pl.BlockSpec((None, tq, WIDTH), lambda bi, i: (bi, i, 0)),
        out_shape=jax.ShapeDtypeStruct((b, t, WIDTH), F32),
        scratch_shapes=[pltpu.VMEM((N_HEADS, tq, MXU_TILE), BF16), pltpu.VMEM((N_HEADS, tq, PAIR), F32),
                        pltpu.VMEM((N_HEADS, tq, 1), F32), pltpu.VMEM((N_HEADS, tq, 1), F32)],
        compiler_params=_cparams(("parallel", "arbitrary"), 40 << 20),
        name="mla_attn",
    )(qn, qr, kcat, vb)


def _layer_norm(y, g, b):
    mu = jnp.mean(y, axis=-1, keepdims=True)
    d = y - mu
    var = jnp.mean(d * d, axis=-1, keepdims=True)
    return d * lax.rsqrt(var + NORM_EPS) * g + b


def _silu(g):
    return g / (1.0 + jnp.exp(-g))


def _finish0_kernel(x_ref, oa_ref, ob_ref, gate_ref, w_ref, g_ref, b_ref, h_ref):
    sg = _silu(gate_ref[...])
    y = (_dot((oa_ref[...] * sg[:, :WIDTH]).astype(BF16), w_ref[:WIDTH, :])
         + _dot((ob_ref[...] * sg[:, WIDTH:]).astype(BF16), w_ref[WIDTH:, :]))
    h_ref[...] = _layer_norm(DN_ALPHA * x_ref[...] + y, g_ref[...], b_ref[...])


def _finish0(x2, oa, ob, gate, w, g, b, tm):
    m = x2.shape[0]
    row = lambda n: pl.BlockSpec((tm, n), lambda i: (i, 0))
    full = lambda a: pl.BlockSpec(a.shape, lambda i: (0,) * a.ndim)
    return pl.pallas_call(
        _finish0_kernel,
        grid=(m // tm,),
        in_specs=[row(D_MODEL), row(WIDTH), row(WIDTH), row(2 * WIDTH), full(w), full(g), full(b)],
        out_specs=row(D_MODEL),
        out_shape=jax.ShapeDtypeStruct((m, D_MODEL), F32),
        compiler_params=_cparams(("parallel",), 40 << 20),
        name="finish0",
    )(x2, oa, ob, gate, w, g, b)


def _finish1_kernel(*refs, n_groups):
    x_ref = refs[0]
    o_refs = refs[1:1 + n_groups]
    l_refs = refs[1 + n_groups:1 + 2 * n_groups] if n_groups > 1 else ()
    gate_ref, w_ref, g_ref, b_ref, h_ref = refs[-5:]
    if n_groups > 1:
        lses = [r[...] for r in l_refs]
        mx = functools.reduce(jnp.maximum, lses)
        es = [jnp.exp(l - mx) for l in lses]
        num = functools.reduce(lambda a, c: a + c, [e * r[...] for e, r in zip(es, o_refs)])
        o = num / functools.reduce(lambda a, c: a + c, es)
    else:
        o = o_refs[0][...]
    y = _dot((o * _silu(gate_ref[...])).astype(BF16), w_ref[...])
    h_ref[...] = _layer_norm(DN_ALPHA * x_ref[...] + y, g_ref[...], b_ref[...])


def _finish1(x2, os_, lses, gate, w, g, b, tm):
    m = x2.shape[0]
    n_groups = len(os_)
    row = lambda n: pl.BlockSpec((tm, n), lambda i: (i, 0))
    full = lambda a: pl.BlockSpec(a.shape, lambda i: (0,) * a.ndim)
    args = [x2, *os_, *lses, gate, w, g, b]
    in_specs = ([row(D_MODEL)] + [row(WIDTH)] * (len(os_) + len(lses)) + [row(WIDTH), full(w), full(g), full(b)])
    return pl.pallas_call(
        functools.partial(_finish1_kernel, n_groups=n_groups),
        grid=(m // tm,),
        in_specs=in_specs,
        out_specs=row(D_MODEL),
        out_shape=jax.ShapeDtypeStruct((m, D_MODEL), F32),
        compiler_params=_cparams(("parallel",), 40 << 20),
        name="finish1",
    )(*args)


def _l1_proj_kernel(x_ref, w_ref, *out_refs):
    xb = x_ref[...].astype(BF16)
    q_refs = out_refs[0:C_GROUPS]
    kvf_refs = out_refs[C_GROUPS:2 * C_GROUPS]
    kvb_refs = out_refs[2 * C_GROUPS:3 * C_GROUPS]
    gate_ref = out_refs[3 * C_GROUPS]
    for g in range(C_GROUPS):
        base = g * 3 * WIDTH
        q_refs[g][...] = (_dot(xb, w_ref[:, base:base + WIDTH]) * ATTN_SCALE).astype(BF16)
        kv = _dot(xb, w_ref[:, base + WIDTH:base + 3 * WIDTH])
        kvf_refs[g][...] = kv
        kvb_refs[g][...] = kv.astype(BF16)
    gate_ref[...] = _dot(xb, w_ref[:, C_GROUPS * 3 * WIDTH:])


def _l1_proj(x2, w1, tm):
    m = x2.shape[0]
    row = lambda n: pl.BlockSpec((tm, n), lambda i: (i, 0))
    outs = ([(WIDTH, BF16)] * C_GROUPS + [(2 * WIDTH, F32)] * C_GROUPS + [(2 * WIDTH, BF16)] * C_GROUPS
            + [(WIDTH, F32)])
    return pl.pallas_call(
        _l1_proj_kernel,
        grid=(m // tm,),
        in_specs=[row(D_MODEL),
                  pl.BlockSpec(w1.shape, lambda i: (0, 0), pipeline_mode=pl.Buffered(1))],
        out_specs=[row(n) for n, _ in outs],
        out_shape=[jax.ShapeDtypeStruct((m, n), dt) for n, dt in outs],
        compiler_params=_cparams(("parallel",), 48 << 20),
        name="l1_proj",
    )(x2, w1)


def _alibi_slope(group, head):
    n = C_GROUPS * N_HEADS
    return float(2.0 ** (-8.0 * (group * N_HEADS + head + 1) / n))


def _dil_attn_kernel(q_ref, kv_ref, o_ref, lse_ref, *, group, n_blocks):
    r = C_DILATIONS[group]
    blk = C_STEPS
    qq = _iota((blk, 2 * blk), 0)
    kk = _iota((blk, 2 * blk), 1)
    steps2 = blk + qq - kk
    valid2 = (steps2 >= 0) & (steps2 <= C_STEPS)
    steps2f = (r * steps2).astype(F32)
    q1 = _iota((blk, blk), 0)
    k1 = _iota((blk, blk), 1)
    valid1 = k1 <= q1
    steps1f = (r * (q1 - k1)).astype(F32)

    def block(n, first):
        qs = pl.multiple_of(n * blk, blk)
        ks = 0 if first else pl.multiple_of((n - 1) * blk, blk)
        kn = blk if first else 2 * blk
        valid, stepsf = (valid1, steps1f) if first else (valid2, steps2f)
        for p in range(N_PAIRS):
            qpair = q_ref[pl.ds(qs, blk), p * PAIR:(p + 1) * PAIR]
            kb = kv_ref[pl.ds(ks, kn), p * PAIR:(p + 1) * PAIR]
            vb = kv_ref[pl.ds(ks, kn), WIDTH + p * PAIR:WIDTH + (p + 1) * PAIR]
            o_h, lse_h = [], []
            for h in (2 * p, 2 * p + 1):
                qp = _keep(_pair_half_mask(blk, h), qpair)
                s = _dot_t(qp, kb) - _alibi_slope(group, h) * stepsf
                s = jnp.where(valid, s, NEG_INF)
                mx = jnp.max(s, axis=1, keepdims=True)
                pr = jnp.exp(s - mx)
                den = jnp.sum(pr, axis=1, keepdims=True)
                o_h.append(_dot(pr.astype(BF16), vb) / den)
                lse_h.append(jnp.broadcast_to(mx + jnp.log(den), (blk, PAIR)))
            even = _pair_half_mask(blk, 0)
            o_ref[pl.ds(qs, blk), p * PAIR:(p + 1) * PAIR] = jnp.where(even, o_h[0], o_h[1])
            lse_ref[pl.ds(qs, blk), p * PAIR:(p + 1) * PAIR] = jnp.where(even, lse_h[0], lse_h[1])

    block(0, True)
    if n_blocks > 1:
        def body(n, c):
            block(n, False)
            return c
        lax.fori_loop(1, n_blocks, body, 0)


def _dil_attn(q, kvb, group):
    b, t, _ = q.shape
    r = C_DILATIONS[group]
    ls = t // r
    qv = q.reshape(b, ls, r * WIDTH)
    kvv = kvb.reshape(b, ls, r * 2 * WIDTH)
    o, lse = pl.pallas_call(
        functools.partial(_dil_attn_kernel, group=group, n_blocks=ls // C_STEPS),
        grid=(b, r),
        in_specs=[pl.BlockSpec((None, ls, WIDTH), lambda bi, s: (bi, 0, s)),
                  pl.BlockSpec((None, ls, 2 * WIDTH), lambda bi, s: (bi, 0, s))],
        out_specs=[pl.BlockSpec((None, ls, WIDTH), lambda bi, s: (bi, 0, s))] * 2,
        out_shape=[jax.ShapeDtypeStruct((b, ls, r * WIDTH), F32)] * 2,
        compiler_params=_cparams(("parallel", "parallel"), 48 << 20),
        name=f"dil_attn{group}",
    )(qv, kvv)
    return o.reshape(b * t, WIDTH), lse.reshape(b * t, WIDTH)


def _block_diag_rows(row, width):
    sel = _head_of_lane((DEC_ROWS, width), 1) == _iota((DEC_ROWS, width), 0)
    return jnp.where(sel, jnp.broadcast_to(row.astype(F32), (DEC_ROWS, width)), 0.0).astype(row.dtype)


def _diag_heads(full):
    sel = _head_of_lane(full.shape, 1) == _iota(full.shape, 0)
    return jnp.sum(jnp.where(sel, full, 0.0), axis=0, keepdims=True)


def _sb_decode_kernel(pt_ref, q_ref, u_ref, *refs, n_pages, chunk):
    page_refs = refs[:n_pages]
    o_ref, qbd_ref, acc_ref, carry_ref = refs[n_pages:]
    s = pl.program_id(1)

    @pl.when(s == 0)
    def _():
        qbd_ref[...] = _block_diag_rows(q_ref[...], WIDTH)
        acc_ref[...] = jnp.zeros_like(acc_ref)
        carry_ref[...] = jnp.zeros_like(carry_ref)

    qbd = qbd_ref[...]
    u = u_ref[...]
    ppc = chunk // LANES
    for c in reversed(range(n_pages // ppc)):
        pages = page_refs[c * ppc:(c + 1) * ppc]
        kb = jnp.concatenate([r[:, :WIDTH] for r in pages], axis=0).astype(BF16)
        vb = jnp.concatenate([r[:, WIDTH:] for r in pages], axis=0).astype(BF16)
        z = _dot_t(qbd, kb)
        sp = _softplus(z)
        lk = -sp
        hi, lo = _split_bf16(lk)
        tail = _dot(hi, u) + _dot(lo, u)
        w = jnp.exp(z - sp + tail + carry_ref[...])
        acc_ref[...] += _dot(w.astype(BF16), vb)
        carry_ref[...] += jnp.sum(lk, axis=1, keepdims=True)

    @pl.when(s == pl.num_programs(1) - 1)
    def _():
        o_ref[...] = _diag_heads(acc_ref[...])


def _sb_decode(page_table, cache, qa_s, umat, n_pages):
    db, pages_per_seq = page_table.shape
    n_steps = pages_per_seq // n_pages
    cache2 = cache.reshape(cache.shape[0], cache.shape[1], 2 * WIDTH)

    def page_spec(k):
        return pl.BlockSpec((None, cache.shape[1], 2 * WIDTH),
                            lambda b, s, pt: (pt[b, (n_steps - 1 - s) * n_pages + k], 0, 0))

    grid_spec = pltpu.PrefetchScalarGridSpec(
        num_scalar_prefetch=1,
        grid=(db, n_steps),
        in_specs=[pl.BlockSpec((None, 1, WIDTH), lambda b, s, pt: (b, 0, 0)),
                  pl.BlockSpec(umat.shape, lambda b, s, pt: (0, 0))] + [page_spec(k) for k in range(n_pages)],
        out_specs=pl.BlockSpec((None, 1, WIDTH), lambda b, s, pt: (b, 0, 0)),
        scratch_shapes=[pltpu.VMEM((DEC_ROWS, WIDTH), BF16), pltpu.VMEM((DEC_ROWS, WIDTH), F32),
                        pltpu.VMEM((DEC_ROWS, 1), F32)],
    )
    return pl.pallas_call(
        functools.partial(_sb_decode_kernel, n_pages=n_pages, chunk=MXU_TILE),
        grid_spec=grid_spec,
        out_shape=jax.ShapeDtypeStruct((db, 1, WIDTH), F32),
        compiler_params=_cparams(("parallel", "arbitrary"), 40 << 20),
        name="sb_decode",
    )(page_table, qa_s.reshape(db, 1, WIDTH), umat, *([cache2] * n_pages))


def _mla_decode_kernel(pt_ref, qn_ref, qr_ref, lat_ref, wuk_ref, wuv_ref, gsel_ref, *refs, n_pages, chunk):
    page_refs = refs[:n_pages]
    o_ref, ql_ref, qrope_ref, acc_ref, m_ref, l_ref = refs[n_pages:]
    s = pl.program_id(1)

    @pl.when(s == 0)
    def _():
        ql = _dot_t(_block_diag_rows(qn_ref[...], WIDTH), wuk_ref[...]).astype(BF16)
        slab_sel = (_iota((DEC_ROWS, 2 * LANES), 1) >= LANES) == (_iota((DEC_ROWS, 2 * LANES), 0) >= 4)
        lane = _iota((DEC_ROWS, 2 * LANES), 1)
        head_sel = (lax.shift_right_logical(lane, 4) & 3) == (_iota((DEC_ROWS, 2 * LANES), 0) & 3)
        valid = _iota((DEC_ROWS, 2 * LANES), 0) < N_HEADS
        qrow = jnp.broadcast_to(qr_ref[...].astype(F32), (DEC_ROWS, 2 * LANES))
        qsel = jnp.where(slab_sel & head_sel & valid, qrow, 0.0).astype(BF16)
        qrope = _dot(qsel, gsel_ref[...]).astype(BF16)
        ql_ref[...] = ql
        qrope_ref[...] = qrope
        own = lat_ref[...].astype(BF16).astype(F32)
        s_own = (jnp.sum(ql.astype(F32) * own[:, :B_KV_RANK], axis=1, keepdims=True)
                 + jnp.sum(qrope.astype(F32) * own[:, B_KV_RANK:], axis=1, keepdims=True)) * MLA_SCALE
        m_ref[...] = s_own
        l_ref[...] = jnp.ones_like(l_ref)
        acc_ref[...] = jnp.broadcast_to(own[:, :B_KV_RANK], acc_ref.shape)

    ql = ql_ref[...]
    qrope = qrope_ref[...]
    ppc = chunk // LANES
    for c in range(n_pages // ppc):
        pages = page_refs[c * ppc:(c + 1) * ppc]
        ckv = jnp.concatenate([r[:, :B_KV_RANK] for r in pages], axis=0).astype(BF16)
        kro = jnp.concatenate([r[:, B_KV_RANK:] for r in pages], axis=0).astype(BF16)
        sc = (_dot_t(ql, ckv) + _dot_t(qrope, kro)) * MLA_SCALE
        m_old = m_ref[...]
        m_new = jnp.maximum(m_old, jnp.max(sc, axis=1, keepdims=True))
        pr = jnp.exp(sc - m_new)
        alpha = jnp.exp(m_old - m_new)
        l_ref[...] = alpha * l_ref[...] + jnp.sum(pr, axis=1, keepdims=True)
        acc_ref[...] = alpha * acc_ref[...] + _dot(pr.astype(BF16), ckv)
        m_ref[...] = m_new

    @pl.when(s == pl.num_programs(1) - 1)
    def _():
        o_lat = (acc_ref[...] / l_ref[...]).astype(BF16)
        o_ref[...] = _diag_heads(_dot(o_lat, wuv_ref[...]))


def _mla_decode(page_table, cache, qn_s, qr_s, lat_s, wuk, wuv, gsel, n_pages):
    db, pages_per_seq = page_table.shape
    n_steps = pages_per_seq // n_pages
    page_rows = cache.shape[1]

    def page_spec(k):
        return pl.BlockSpec((None, page_rows, B_LATENT), lambda b, s, pt: (pt[b, s * n_pages + k], 0, 0))

    per_b = lambda n: pl.BlockSpec((None, 1, n), lambda b, s, pt: (b, 0, 0))
    full = lambda a: pl.BlockSpec(a.shape, lambda b, s, pt: (0,) * a.ndim)
    grid_spec = pltpu.PrefetchScalarGridSpec(
        num_scalar_prefetch=1,
        grid=(db, n_steps),
        in_specs=[per_b(WIDTH), per_b(2 * LANES), per_b(B_LATENT), full(wuk), full(wuv), full(gsel)]
                 + [page_spec(k) for k in range(n_pages)],
        out_specs=per_b(WIDTH),
        scratch_shapes=[pltpu.VMEM((DEC_ROWS, B_KV_RANK), BF16), pltpu.VMEM((DEC_ROWS, B_ROPE), BF16),
                        pltpu.VMEM((DEC_ROWS, B_KV_RANK), F32), pltpu.VMEM((DEC_ROWS, 1), F32),
                        pltpu.VMEM((DEC_ROWS, 1), F32)],
    )
    return pl.pallas_call(
        functools.partial(_mla_decode_kernel, n_pages=n_pages, chunk=MXU_TILE),
        grid_spec=grid_spec,
        out_shape=jax.ShapeDtypeStruct((db, 1, WIDTH), F32),
        compiler_params=_cparams(("parallel", "arbitrary"), 40 << 20),
        name="mla_decode",
    )(page_table, qn_s.reshape(db, 1, WIDTH), qr_s.reshape(db, 1, 2 * LANES), lat_s.reshape(db, 1, B_LATENT),
      wuk, wuv, gsel, *([cache] * n_pages))


def _dil_decode_kernel(q_ref, kvn_ref, c0_ref, c1_ref, c2_ref, o_ref):
    caches = (c0_ref, c1_ref, c2_ref)
    head = _iota((DEC_ROWS, 1), 0).astype(F32)
    back = (C_STEPS - _iota((DEC_ROWS, C_STEPS), 1)).astype(F32)
    outs, lses = [], []
    for g in range(C_GROUPS):
        r = C_DILATIONS[g]
        slope = jnp.exp((-8.0 * (g * N_HEADS + head + 1.0) / (C_GROUPS * N_HEADS)) * np.log(2.0))
        qbd = _block_diag_rows(q_ref[:, g * WIDTH:(g + 1) * WIDTH], WIDTH)
        kb = caches[g][:, :WIDTH].astype(BF16)
        vb = caches[g][:, WIDTH:].astype(BF16)
        k_new = kvn_ref[:, g * 2 * WIDTH:g * 2 * WIDTH + WIDTH].astype(BF16)
        v_new = kvn_ref[:, g * 2 * WIDTH + WIDTH:(g + 1) * 2 * WIDTH].astype(BF16).astype(F32)
        s_win = _dot_t(qbd, kb) - slope * (r * back)
        s_own = jnp.sum(qbd.astype(F32) * k_new.astype(F32), axis=1, keepdims=True)
        mx = jnp.maximum(jnp.max(s_win, axis=1, keepdims=True), s_own)
        p_win = jnp.exp(s_win - mx)
        p_own = jnp.exp(s_own - mx)
        den = jnp.sum(p_win, axis=1, keepdims=True) + p_own
        o = (_dot(p_win.astype(BF16), vb) + p_own.astype(BF16).astype(F32) * v_new) / den
        outs.append(o)
        lses.append(mx + jnp.log(den))
    mx = functools.reduce(jnp.maximum, lses)
    es = [jnp.exp(l - mx) for l in lses]
    tot = functools.reduce(lambda a, c: a + c, es)
    merged = functools.reduce(lambda a, c: a + c, [(e / tot) * o for e, o in zip(es, outs)])
    o_ref[...] = _diag_heads(merged)


def _dil_decode(q3, kvn3, caches):
    db = q3.shape[0]
    per_b = lambda n: pl.BlockSpec((None, 1, n), lambda b: (b, 0, 0))
    views, specs = [], []
    for g, c in enumerate(caches):
        r = C_DILATIONS[g]
        assert c.shape[1] == C_STEPS * r
        views.append(c.reshape(db, C_STEPS, r * 2 * WIDTH))
        specs.append(pl.BlockSpec((None, C_STEPS, 2 * WIDTH), lambda b: (b, 0, 0)))
    return pl.pallas_call(
        _dil_decode_kernel,
        grid=(db,),
        in_specs=[per_b(C_GROUPS * WIDTH), per_b(C_GROUPS * 2 * WIDTH)] + specs,
        out_specs=per_b(WIDTH),
        out_shape=jax.ShapeDtypeStruct((db, 1, WIDTH), F32),
        compiler_params=_cparams(("parallel",), 32 << 20),
        name="dil_decode",
    )(q3.reshape(db, 1, C_GROUPS * WIDTH), kvn3.reshape(db, 1, C_GROUPS * 2 * WIDTH), *views)


def _prep_w_in0(w_in0):
    cuts = np.cumsum(IN0_SIZES)[:-1].tolist()
    qa, ka, va, dq, dkv, kr, gate = jnp.split(w_in0, cuts, axis=1)
    kr_sw = jnp.concatenate([kr[:, B_HALF:], kr[:, :B_HALF]], axis=1)
    return jnp.concatenate([qa, ka, va, gate, dq, dkv, kr, kr_sw], axis=1).astype(BF16)


def _prep_w_uq(w_uq):
    w = w_uq.reshape(B_Q_RANK, N_HEADS, B_NOPE + B_ROPE)
    nope = w[:, :, :B_NOPE].reshape(B_Q_RANK, WIDTH)
    x1 = w[:, :, B_NOPE:B_NOPE + B_HALF]
    x2 = w[:, :, B_NOPE + B_HALF:]
    slabs, slabs_sw = [], []
    for s in range(2):
        a = x1[:, 4 * s:4 * s + 4].reshape(B_Q_RANK, 4 * B_HALF)
        c = x2[:, 4 * s:4 * s + 4].reshape(B_Q_RANK, 4 * B_HALF)
        slabs.append(jnp.concatenate([a, c], axis=1))
        slabs_sw.append(jnp.concatenate([c, a], axis=1))
    return jnp.concatenate([nope] + slabs + slabs_sw, axis=1).astype(BF16)


def _rope_table(pos):
    inv = ROPE_THETA ** (-jnp.arange(0, B_ROPE, 2, dtype=F32) / B_ROPE)
    ang = pos.astype(F32)[:, None] * inv[None, :]
    cos, sin = jnp.cos(ang), jnp.sin(ang)
    n = pos.shape[0]
    pad = jnp.zeros((n, LANES - B_ROPE), F32)
    return jnp.concatenate([jnp.tile(cos, (1, 8)),
                            jnp.tile(-sin, (1, 4)), jnp.tile(sin, (1, 4)),
                            cos, cos, pad, -sin, sin, pad], axis=1)


def _const_mats():
    umat = (np.arange(MXU_TILE)[:, None] > np.arange(MXU_TILE)[None, :]).astype(np.float32)
    emat = np.zeros((B_ROPE, LANES), np.float32)
    for c in range(LANES):
        emat[(c % B_HALF) + (B_HALF if c >= 4 * B_HALF else 0), c] = 1.0
    gsel = np.zeros((2 * LANES, B_ROPE), np.float32)
    for c in range(2 * LANES):
        cc = c % LANES
        gsel[c, (cc % B_HALF) + (B_HALF if cc >= 4 * B_HALF else 0)] = 1.0
    return jnp.asarray(umat, BF16), jnp.asarray(emat, BF16), jnp.asarray(gsel, BF16)


def kernel(x_prompt, x_sample, cache_a_kv, cache_b_latent, cache_c0_kv, cache_c1_kv, cache_c2_kv,
           page_table, w_in0, g_cq, g_ckv, w_uq, w_uk, w_uv, w_out0, ln0_g, ln0_b,
           w_in1, w_out1, ln1_g, ln1_b):
    bsz, t_p, _ = x_prompt.shape
    db, t_s, _ = x_sample.shape
    assert t_s == 1 and t_p % MXU_TILE == 0
    past_len = page_table.shape[1] * cache_a_kv.shape[1]
    m_p, m_s = bsz * t_p, db * t_s

    umat, emat, gsel = _const_mats()
    w0 = _prep_w_in0(w_in0)
    wuq = _prep_w_uq(w_uq)
    wuk_b, wuv_b = w_uk.astype(BF16), w_uv.astype(BF16)
    wo0, wo1, w1 = w_out0.astype(BF16), w_out1.astype(BF16), w_in1.astype(BF16)
    row = lambda v: v.reshape(1, -1)
    tab_p = _rope_table(jnp.arange(t_p))
    tab_s = _rope_table(jnp.full((m_s,), past_len, jnp.int32))

    xp = x_prompt.reshape(m_p, D_MODEL)
    xs = x_sample.reshape(m_s, D_MODEL)
    tm_p, tm_s = 256, m_s

    proj = lambda x2, tab, tm: _l0_proj(x2, tab, w0, row(g_cq), row(g_ckv), wuq, wuk_b, wuv_b, emat, tm)
    qa_p, kvf_p, kvb_p, gate_p, lat_p, qn_p, qr_p, kcat_p, vb_p = proj(xp, tab_p, tm_p)
    qa_s, kvf_s, _, gate_s, lat_s, qn_s, qr_s, _, _ = proj(xs, tab_s, tm_s)

    r3 = lambda a: a.reshape(bsz, t_p, a.shape[-1])
    oa_p = _sb_attn(r3(qa_p), r3(kvb_p), umat, MXU_TILE).reshape(m_p, WIDTH)
    ob_p = _mla_attn(r3(qn_p), r3(qr_p), r3(kcat_p), r3(vb_p), MXU_TILE).reshape(m_p, WIDTH)
    h_p = _finish0(xp, oa_p, ob_p, gate_p, wo0, row(ln0_g), row(ln0_b), tm_p)

    oa_s = _sb_decode(page_table, cache_a_kv, qa_s, umat, n_pages=16).reshape(m_s, WIDTH)
    ob_s = _mla_decode(page_table, cache_b_latent, qn_s, qr_s, lat_s, wuk_b, wuv_b, gsel,
                       n_pages=32).reshape(m_s, WIDTH)
    h_s = _finish0(xs, oa_s, ob_s, gate_s, wo0, row(ln0_g), row(ln0_b), tm_s)

    outs_p = _l1_proj(h_p, w1, tm_p)
    q_p, kvf1_p, kvb1_p, gate1_p = outs_p[0:3], outs_p[3:6], outs_p[6:9], outs_p[9]
    os_p, lses_p = [], []
    for g in range(C_GROUPS):
        o, lse = _dil_attn(r3(q_p[g]), r3(kvb1_p[g]), g)
        os_p.append(o)
        lses_p.append(lse)
    y_p = _finish1(h_p, os_p, lses_p, gate1_p, wo1, row(ln1_g), row(ln1_b), tm_p)

    outs_s = _l1_proj(h_s, w1, tm_s)
    q_s, kvf1_s, gate1_s = outs_s[0:3], outs_s[3:6], outs_s[9]
    oc_s = _dil_decode(jnp.concatenate(q_s, axis=1), jnp.concatenate(kvf1_s, axis=1),
                       (cache_c0_kv, cache_c1_kv, cache_c2_kv)).reshape(m_s, WIDTH)
    y_s = _finish1(h_s, [oc_s], [], gate1_s, wo1, row(ln1_g), row(ln1_b), tm_s)

    kv5 = lambda a, n, t: a.reshape(n, t, 2, N_HEADS, HEAD_DIM)
    c_new_p = []
    for g, c in enumerate((cache_c0_kv, cache_c1_kv, cache_c2_kv)):
        win = min(c.shape[1], t_p)
        c_new_p.append(kv5(kvf1_p[g], bsz, t_p)[:, t_p - win:])
    return (y_p.reshape(bsz, t_p, D_MODEL), y_s.reshape(db, t_s, D_MODEL),
            kv5(kvf_p, bsz, t_p), kv5(kvf_s, db, t_s),
            lat_p.reshape(bsz, t_p, B_LATENT), lat_s.reshape(db, t_s, B_LATENT),
            c_new_p[0], kv5(kvf1_s[0], db, t_s), c_new_p[1], kv5(kvf1_s[1], db, t_s),
            c_new_p[2], kv5(kvf1_s[2], db, t_s))
```

```python
import functools

import numpy as np
import jax
import jax.numpy as jnp
from jax import lax
from jax.experimental import pallas as pl
from jax.experimental.pallas import tpu as pltpu

D_MODEL = 1024
DEPTH = 2
HEAD_DIM = 64
N_HEADS = 8
WIDTH = N_HEADS * HEAD_DIM
ATTN_SCALE = HEAD_DIM ** -0.5
B_NOPE = 64
B_ROPE = 32
B_HALF = B_ROPE // 2
B_Q_RANK = 384
B_KV_RANK = 256
B_LATENT = B_KV_RANK + B_ROPE
MLA_SCALE = (B_NOPE + B_ROPE) ** -0.5
ROPE_THETA = 10000.0
C_DILATIONS = (1, 4, 16)
C_GROUPS = 3
C_STEPS = 128
NORM_EPS = 1e-5
DN_ALPHA = (2 * DEPTH) ** 0.25
IN0_SIZES = (WIDTH, WIDTH, WIDTH, B_Q_RANK, B_KV_RANK, B_ROPE, 2 * WIDTH)

LANES = 128
PAIR = 2 * HEAD_DIM
N_PAIRS = N_HEADS // 2
V7X_VMEM_BYTES = 64 * 1024 * 1024
MXU_TILE = 256
DEC_ROWS = 16

NEG_INF = float("-inf")
SB_EXIT = -104.0
F32 = jnp.float32
BF16 = jnp.bfloat16


def _cparams(semantics, vmem_bytes):
    return pltpu.CompilerParams(dimension_semantics=semantics,
                                vmem_limit_bytes=min(int(vmem_bytes), V7X_VMEM_BYTES - (4 << 20)))


def _dot(a, b):
    return jnp.dot(a, b, preferred_element_type=F32)


def _dot_t(a, b):
    return lax.dot_general(a, b, (((1,), (1,)), ((), ())), preferred_element_type=F32)


def _iota(shape, dim):
    return lax.broadcasted_iota(jnp.int32, shape, dim)


def _softplus(z):
    return jnp.maximum(z, 0.0) + jnp.log(1.0 + jnp.exp(-jnp.abs(z)))


def _split_bf16(x):
    hi = x.astype(BF16)
    lo = (x - hi.astype(F32)).astype(BF16)
    return hi, lo


def _pair_half_mask(rows, head):
    lane = _iota((rows, PAIR), 1)
    return (lane >= HEAD_DIM) if head % 2 else (lane < HEAD_DIM)


def _keep(mask, x):
    return jnp.where(mask, x.astype(F32), 0.0).astype(x.dtype)


def _head_of_lane(shape, dim):
    return lax.shift_right_logical(_iota(shape, dim), 6)


def _twice(x):
    return jnp.concatenate([x, x], axis=1)


_Q_QA, _Q_GATE, _Q_DQ, _N_Q0 = 0, WIDTH, 3 * WIDTH, 3 * WIDTH + B_Q_RANK
_K_KV, _K_DKV, _K_KR, _N_K0 = 0, 2 * WIDTH, 2 * WIDTH + B_KV_RANK, 2 * WIDTH + B_KV_RANK + 2 * B_ROPE
_N_UQ = WIDTH + 4 * LANES


def _l0_proj_kernel(x_ref, tabq_ref, tabk_ref, wq_ref, wk_ref, gcq_ref, gckv_ref, wuq_ref, wuk_ref, wuv_ref,
                    e_ref, qa_ref, gate_ref, qn_ref, qr_ref, kvf_ref, lat_ref, kvb_ref, kcat_ref, vb_ref):
    xb = x_ref[...].astype(BF16)
    qa_ref[...] = (_dot(xb, wq_ref[:, _Q_QA:_Q_GATE]) * ATTN_SCALE).astype(BF16)
    gate_ref[...] = _dot(xb, wq_ref[:, _Q_GATE:_Q_DQ])
    dq = _dot(xb, wq_ref[:, _Q_DQ:_N_Q0])
    dqn = dq * lax.rsqrt(jnp.mean(dq * dq, axis=-1, keepdims=True) + NORM_EPS) * gcq_ref[...]
    qf = _dot(dqn.astype(BF16), wuq_ref[...])
    qn_ref[...] = qf[:, :WIDTH].astype(BF16)
    cosq = tabq_ref[:, :LANES]
    sinq = tabq_ref[:, LANES:]
    for s in range(2):
        nat = qf[:, WIDTH + s * LANES:WIDTH + (s + 1) * LANES]
        swp = qf[:, WIDTH + (2 + s) * LANES:WIDTH + (3 + s) * LANES]
        qr_ref[:, s * LANES:(s + 1) * LANES] = (nat * cosq + swp * sinq).astype(BF16)
    kv = _dot_t(wk_ref[_K_KV:_K_DKV, :], xb)
    kvf_ref[...] = kv
    kvb_ref[...] = kv.astype(BF16)
    dkv = _dot_t(wk_ref[_K_DKV:_K_KR, :], xb)
    ckv = dkv * lax.rsqrt(jnp.mean(dkv * dkv, axis=0, keepdims=True) + NORM_EPS) * gckv_ref[...]
    lat_ref[:B_KV_RANK, :] = ckv
    kr2 = _dot_t(wk_ref[_K_KR:_N_K0, :], xb)
    krot = kr2[:B_ROPE, :] * tabk_ref[:B_ROPE, :] + kr2[B_ROPE:, :] * tabk_ref[B_ROPE:, :]
    lat_ref[B_KV_RANK:, :] = krot
    ckv_b = ckv.astype(BF16)
    kn = _dot(wuk_ref[...], ckv_b).astype(BF16)
    vb_ref[...] = _dot(wuv_ref[...], ckv_b).astype(BF16)
    kslab = _dot(e_ref[...], krot.astype(BF16)).astype(BF16)
    for p in range(N_PAIRS):
        kcat_ref[p * MXU_TILE:p * MXU_TILE + PAIR, :] = kn[p * PAIR:(p + 1) * PAIR, :]
        kcat_ref[p * MXU_TILE + PAIR:(p + 1) * MXU_TILE, :] = kslab


def _l0_proj(x3, tabq, tabk, wq, wk, gcq, gckv, wuq, wuk_t, wuv_t, emat_t, tm):
    b, t, _ = x3.shape
    nt = t // tm
    tok = lambda n: pl.BlockSpec((None, tm, n), lambda bi, ti: (bi, ti, 0))
    feat = lambda n: pl.BlockSpec((None, n, tm), lambda bi, ti: (bi, 0, ti))
    featb = lambda n: pl.BlockSpec((None, None, n, tm), lambda bi, ti: (bi, ti, 0, 0))
    full = lambda a: pl.BlockSpec(a.shape, lambda bi, ti: (0,) * a.ndim, pipeline_mode=pl.Buffered(1))
    s = jax.ShapeDtypeStruct
    return pl.pallas_call(
        _l0_proj_kernel,
        grid=(b, nt),
        in_specs=[tok(D_MODEL), pl.BlockSpec((tm, 2 * LANES), lambda bi, ti: (ti, 0)),
                  pl.BlockSpec((2 * B_ROPE, tm), lambda bi, ti: (0, ti)),
                  full(wq), full(wk), full(gcq), full(gckv), full(wuq), full(wuk_t), full(wuv_t), full(emat_t)],
        out_specs=[tok(WIDTH), tok(2 * WIDTH), tok(WIDTH), tok(2 * LANES),
                   feat(2 * WIDTH), feat(B_LATENT),
                   featb(2 * WIDTH), featb(N_PAIRS * MXU_TILE), featb(WIDTH)],
        out_shape=[s((b, t, WIDTH), BF16), s((b, t, 2 * WIDTH), F32), s((b, t, WIDTH), BF16),
                   s((b, t, 2 * LANES), BF16),
                   s((b, 2 * WIDTH, t), F32), s((b, B_LATENT, t), F32),
                   s((b, nt, 2 * WIDTH, tm), BF16), s((b, nt, N_PAIRS * MXU_TILE, tm), BF16),
                   s((b, nt, WIDTH, tm), BF16)],
        compiler_params=_cparams(("parallel", "parallel"), 48 << 20),
        name="l0_proj",
    )(x3, tabq, tabk, wq, wk, gcq, gckv, wuq, wuk_t, wuv_t, emat_t)


def _sb_attn_kernel(q_ref, kv_ref, u_ref, o_ref, qp_ref, acc_ref, carry_ref, *, tq):
    i = pl.program_id(1)
    acc_ref[...] = jnp.zeros_like(acc_ref)
    carry_ref[...] = jnp.zeros_like(carry_ref)
    for h in range(N_HEADS):
        qp_ref[h] = _keep(_pair_half_mask(tq, h), q_ref[:, (h // 2) * PAIR:(h // 2 + 1) * PAIR])

    def block(j, masked):
        u = u_ref[...]
        if masked:
            strict = _iota((tq, tq), 1) < _iota((tq, tq), 0)
        for h in range(N_HEADS):
            p = h // 2
            kt = kv_ref[j, p * PAIR:(p + 1) * PAIR, :]
            vt = kv_ref[j, WIDTH + p * PAIR:WIDTH + (p + 1) * PAIR, :]
            z = _dot(qp_ref[h], kt)
            sp = _softplus(z)
            lk = jnp.where(strict, -sp, 0.0) if masked else -sp
            hi, lo = _split_bf16(lk)
            tail = _dot(hi, u) + _dot(lo, u)
            w = jnp.exp(z - sp + tail + _twice(carry_ref[h]))
            if masked:
                w = jnp.where(strict, w, 0.0)
            acc_ref[h] += _dot_t(w.astype(BF16), vt)
            carry_ref[h] += jnp.sum(lk, axis=1, keepdims=True)

    def least_decayed():
        m = carry_ref[0]
        for h in range(1, N_HEADS):
            m = jnp.maximum(m, carry_ref[h])
        return jnp.max(m)

    block(i, True)

    def cond(c):
        jj, mx = c
        return (jj <= i) & (mx >= SB_EXIT)

    def body(c):
        jj, _ = c
        block(i - jj, False)
        return jj + 1, least_decayed()

    lax.while_loop(cond, body, (jnp.int32(1), least_decayed()))
    for p in range(N_PAIRS):
        o_ref[:, p * PAIR:(p + 1) * PAIR] = jnp.where(_pair_half_mask(tq, 0), acc_ref[2 * p], acc_ref[2 * p + 1])


def _sb_attn(qa, kvb, umat):
    b, t, _ = qa.shape
    _, nblk, _, tq = kvb.shape
    return pl.pallas_call(
        functools.partial(_sb_attn_kernel, tq=tq),
        grid=(b, nblk),
        in_specs=[pl.BlockSpec((None, tq, WIDTH), lambda bi, i: (bi, i, 0)),
                  pl.BlockSpec((None, nblk, 2 * WIDTH, tq), lambda bi, i: (bi, 0, 0, 0)),
                  pl.BlockSpec(umat.shape, lambda bi, i: (0, 0))],
        out_specs=pl.BlockSpec((None, tq, WIDTH), lambda bi, i: (bi, i, 0)),
        out_shape=jax.ShapeDtypeStruct((b, t, WIDTH), F32),
        scratch_shapes=[pltpu.VMEM((N_HEADS, tq, PAIR), BF16), pltpu.VMEM((N_HEADS, tq, PAIR), F32),
                        pltpu.VMEM((N_HEADS, tq, PAIR), F32)],
        compiler_params=_cparams(("parallel", "arbitrary"), 40 << 20),
        name="sb_attn",
    )(qa, kvb, umat)


def _rope_slab_mask(rows, head):
    lane = _iota((rows, LANES), 1)
    return (lax.shift_right_logical(lane, 4) & 3) == (head % 4)


def _mla_attn_kernel(qn_ref, qr_ref, kcat_ref, vb_ref, o_ref, qc_ref, acc_ref, m_ref, l_ref, *, tq):
    i = pl.program_id(1)
    acc_ref[...] = jnp.zeros_like(acc_ref)
    m_ref[...] = jnp.full_like(m_ref, NEG_INF)
    l_ref[...] = jnp.zeros_like(l_ref)
    for h in range(N_HEADS):
        qc_ref[h, :, :PAIR] = _keep(_pair_half_mask(tq, h), qn_ref[:, (h // 2) * PAIR:(h // 2 + 1) * PAIR])
        qc_ref[h, :, PAIR:] = _keep(_rope_slab_mask(tq, h), qr_ref[:, (h // 4) * LANES:(h // 4 + 1) * LANES])

    def block(j, masked):
        if masked:
            causal = _iota((tq, tq), 1) <= _iota((tq, tq), 0)
        for h in range(N_HEADS):
            p = h // 2
            kt = kcat_ref[j, p * MXU_TILE:(p + 1) * MXU_TILE, :]
            vt = vb_ref[j, p * PAIR:(p + 1) * PAIR, :]
            s = _dot(qc_ref[h], kt) * MLA_SCALE
            if masked:
                s = jnp.where(causal, s, NEG_INF)
            m_old = m_ref[h]
            m_new = jnp.maximum(m_old, jnp.max(s, axis=1, keepdims=True))
            pr = jnp.exp(s - _twice(m_new))
            alpha = jnp.exp(m_old - m_new)
            l_ref[h] = alpha * l_ref[h] + jnp.sum(pr, axis=1, keepdims=True)
            acc_ref[h] = alpha * acc_ref[h] + _dot_t(pr.astype(BF16), vt)
            m_ref[h] = m_new

    block(i, True)

    def body(j, c):
        block(j, False)
        return c

    lax.fori_loop(0, i, body, 0)
    for p in range(N_PAIRS):
        o_ref[:, p * PAIR:(p + 1) * PAIR] = jnp.where(
            _pair_half_mask(tq, 0), acc_ref[2 * p] / l_ref[2 * p], acc_ref[2 * p + 1] / l_ref[2 * p + 1])


def _mla_attn(qn, qr, kcat, vb):
    b, t, _ = qn.shape
    _, nblk, _, tq = kcat.shape
    return pl.pallas_call(
        functools.partial(_mla_attn_kernel, tq=tq),
        grid=(b, nblk),
        in_specs=[pl.BlockSpec((None, tq, WIDTH), lambda bi, i: (bi, i, 0)),
                  pl.BlockSpec((None, tq, 2 * LANES), lambda bi, i: (bi, i, 0)),
                  pl.BlockSpec((None, nblk, N_PAIRS * MXU_TILE, tq), lambda bi, i: (bi, 0, 0, 0)),
                  pl.BlockSpec((None, nblk, WIDTH, tq), lambda bi, i: (bi, 0, 0, 0))],
        out_specs=pl.BlockSpec((None, tq, WIDTH), lambda bi, i: (bi, i, 0)),
        out_shape=jax.ShapeDtypeStruct((b, t, WIDTH), F32),
        scratch_shapes=[pltpu.VMEM((N_HEADS, tq, MXU_TILE), BF16), pltpu.VMEM((N_HEADS, tq, PAIR), F32),
                        pltpu.VMEM((N_HEADS, tq, PAIR), F32), pltpu.VMEM((N_HEADS, tq, PAIR), F32)],
        compiler_params=_cparams(("parallel", "arbitrary"), 40 << 20),
        name="mla_attn",
    )(qn, qr, kcat, vb)


def _layer_norm(y, g, b):
    mu = jnp.mean(y, axis=-1, keepdims=True)
    d = y - mu
    var = jnp.mean(d * d, axis=-1, keepdims=True)
    return d * lax.rsqrt(var + NORM_EPS) * g + b


def _silu(g):
    return g / (1.0 + jnp.exp(-g))


def _finish0_kernel(x_ref, oa_ref, ob_ref, gate_ref, w_ref, g_ref, b_ref, h_ref):
    sg = _silu(gate_ref[...])
    y = (_dot((oa_ref[...] * sg[:, :WIDTH]).astype(BF16), w_ref[:WIDTH, :])
         + _dot((ob_ref[...] * sg[:, WIDTH:]).astype(BF16), w_ref[WIDTH:, :]))
    h_ref[...] = _layer_norm(DN_ALPHA * x_ref[...] + y, g_ref[...], b_ref[...])


def _finish0(x2, oa, ob, gate, w, g, b, tm):
    m = x2.shape[0]
    row = lambda n: pl.BlockSpec((tm, n), lambda i: (i, 0))
    full = lambda a: pl.BlockSpec(a.shape, lambda i: (0,) * a.ndim)
    return pl.pallas_call(
        _finish0_kernel,
        grid=(m // tm,),
        in_specs=[row(D_MODEL), row(WIDTH), row(WIDTH), row(2 * WIDTH), full(w), full(g), full(b)],
        out_specs=row(D_MODEL),
        out_shape=jax.ShapeDtypeStruct((m, D_MODEL), F32),
        compiler_params=_cparams(("parallel",), 40 << 20),
        name="finish0",
    )(x2, oa, ob, gate, w, g, b)


def _finish1_kernel(*refs, n_groups):
    x_ref = refs[0]
    o_refs = refs[1:1 + n_groups]
    l_refs = refs[1 + n_groups:1 + 2 * n_groups] if n_groups > 1 else ()
    gate_ref, w_ref, g_ref, b_ref, h_ref = refs[-5:]
    if n_groups > 1:
        lses = [r[...] for r in l_refs]
        mx = functools.reduce(jnp.maximum, lses)
        es = [jnp.exp(l - mx) for l in lses]
        num = functools.reduce(lambda a, c: a + c, [e * r[...] for e, r in zip(es, o_refs)])
        o = num / functools.reduce(lambda a, c: a + c, es)
    else:
        o = o_refs[0][...]
    y = _dot((o * _silu(gate_ref[...])).astype(BF16), w_ref[...])
    h_ref[...] = _layer_norm(DN_ALPHA * x_ref[...] + y, g_ref[...], b_ref[...])


def _finish1(x2, os_, lses, gate, w, g, b, tm):
    m = x2.shape[0]
    n_groups = len(os_)
    row = lambda n: pl.BlockSpec((tm, n), lambda i: (i, 0))
    full = lambda a: pl.BlockSpec(a.shape, lambda i: (0,) * a.ndim)
    args = [x2, *os_, *lses, gate, w, g, b]
    in_specs = ([row(D_MODEL)] + [row(WIDTH)] * (len(os_) + len(lses)) + [row(WIDTH), full(w), full(g), full(b)])
    return pl.pallas_call(
        functools.partial(_finish1_kernel, n_groups=n_groups),
        grid=(m // tm,),
        in_specs=in_specs,
        out_specs=row(D_MODEL),
        out_shape=jax.ShapeDtypeStruct((m, D_MODEL), F32),
        compiler_params=_cparams(("parallel",), 40 << 20),
        name="finish1",
    )(*args)


def _l1_proj_kernel(x_ref, w_ref, *out_refs, kv_dtype):
    xb = x_ref[...].astype(BF16)
    q_refs = out_refs[0:C_GROUPS]
    kv_refs = out_refs[C_GROUPS:2 * C_GROUPS]
    gate_ref = out_refs[2 * C_GROUPS]
    for g in range(C_GROUPS):
        base = g * 3 * WIDTH
        q_refs[g][...] = (_dot(xb, w_ref[:, base:base + WIDTH]) * ATTN_SCALE).astype(BF16)
        kv_refs[g][...] = _dot(xb, w_ref[:, base + WIDTH:base + 3 * WIDTH]).astype(kv_dtype)
    gate_ref[...] = _dot(xb, w_ref[:, C_GROUPS * 3 * WIDTH:])


def _l1_proj(x2, w1, tm, kv_dtype):
    m = x2.shape[0]
    row = lambda n: pl.BlockSpec((tm, n), lambda i: (i, 0))
    outs = [(WIDTH, BF16)] * C_GROUPS + [(2 * WIDTH, kv_dtype)] * C_GROUPS + [(WIDTH, F32)]
    return pl.pallas_call(
        functools.partial(_l1_proj_kernel, kv_dtype=kv_dtype),
        grid=(m // tm,),
        in_specs=[row(D_MODEL),
                  pl.BlockSpec(w1.shape, lambda i: (0, 0), pipeline_mode=pl.Buffered(1))],
        out_specs=[row(n) for n, _ in outs],
        out_shape=[jax.ShapeDtypeStruct((m, n), dt) for n, dt in outs],
        compiler_params=_cparams(("parallel",), 48 << 20),
        name="l1_proj",
    )(x2, w1)


def _proj_t_kernel(x_ref, w_ref, o_ref):
    o_ref[...] = _dot_t(w_ref[...], x_ref[...].astype(BF16))


def _proj_t(x3, w_t, window, tm):
    b, t, k = x3.shape
    f = w_t.shape[0]
    first = (t - window) // tm
    return pl.pallas_call(
        _proj_t_kernel,
        grid=(b, window // tm),
        in_specs=[pl.BlockSpec((None, tm, k), lambda bi, ti: (bi, first + ti, 0)),
                  pl.BlockSpec(w_t.shape, lambda bi, ti: (0, 0))],
        out_specs=pl.BlockSpec((None, f, tm), lambda bi, ti: (bi, 0, ti)),
        out_shape=jax.ShapeDtypeStruct((b, f, window), F32),
        compiler_params=_cparams(("parallel", "parallel"), 32 << 20),
        name="proj_t",
    )(x3, w_t)


def _alibi_slope(group, head):
    n = C_GROUPS * N_HEADS
    return float(2.0 ** (-8.0 * (group * N_HEADS + head + 1) / n))


def _dil_attn_kernel(q_ref, kv_ref, o_ref, lse_ref, *, group, n_blocks):
    r = C_DILATIONS[group]
    blk = C_STEPS
    qq = _iota((blk, 2 * blk), 0)
    kk = _iota((blk, 2 * blk), 1)
    steps2 = blk + qq - kk
    valid2 = (steps2 >= 0) & (steps2 <= C_STEPS)
    steps2f = (r * steps2).astype(F32)
    q1 = _iota((blk, blk), 0)
    k1 = _iota((blk, blk), 1)
    valid1 = k1 <= q1
    steps1f = (r * (q1 - k1)).astype(F32)

    def block(n, first):
        qs = pl.multiple_of(n * blk, blk)
        ks = 0 if first else pl.multiple_of((n - 1) * blk, blk)
        kn = blk if first else 2 * blk
        valid, stepsf = (valid1, steps1f) if first else (valid2, steps2f)
        for p in range(N_PAIRS):
            qpair = q_ref[pl.ds(qs, blk), p * PAIR:(p + 1) * PAIR]
            kb = kv_ref[pl.ds(ks, kn), p * PAIR:(p + 1) * PAIR]
            vb = kv_ref[pl.ds(ks, kn), WIDTH + p * PAIR:WIDTH + (p + 1) * PAIR]
            o_h, lse_h = [], []
            for h in (2 * p, 2 * p + 1):
                qp = _keep(_pair_half_mask(blk, h), qpair)
                s = _dot_t(qp, kb) - _alibi_slope(group, h) * stepsf
                s = jnp.where(valid, s, NEG_INF)
                mx = jnp.max(s, axis=1, keepdims=True)
                pr = jnp.exp(s - mx)
                den = jnp.sum(pr, axis=1, keepdims=True)
                o_h.append(_dot(pr.astype(BF16), vb) / den)
                lse_h.append(jnp.broadcast_to(mx + jnp.log(den), (blk, PAIR)))
            even = _pair_half_mask(blk, 0)
            o_ref[pl.ds(qs, blk), p * PAIR:(p + 1) * PAIR] = jnp.where(even, o_h[0], o_h[1])
            lse_ref[pl.ds(qs, blk), p * PAIR:(p + 1) * PAIR] = jnp.where(even, lse_h[0], lse_h[1])

    block(0, True)
    if n_blocks > 1:
        def body(n, c):
            block(n, False)
            return c
        lax.fori_loop(1, n_blocks, body, 0)


def _dil_attn(q, kvb, group):
    b, t, _ = q.shape
    r = C_DILATIONS[group]
    ls = t // r
    qv = q.reshape(b, ls, r * WIDTH)
    kvv = kvb.reshape(b, ls, r * 2 * WIDTH)
    o, lse = pl.pallas_call(
        functools.partial(_dil_attn_kernel, group=group, n_blocks=ls // C_STEPS),
        grid=(b, r),
        in_specs=[pl.BlockSpec((None, ls, WIDTH), lambda bi, s: (bi, 0, s)),
                  pl.BlockSpec((None, ls, 2 * WIDTH), lambda bi, s: (bi, 0, s))],
        out_specs=[pl.BlockSpec((None, ls, WIDTH), lambda bi, s: (bi, 0, s))] * 2,
        out_shape=[jax.ShapeDtypeStruct((b, ls, r * WIDTH), F32)] * 2,
        compiler_params=_cparams(("parallel", "parallel"), 48 << 20),
        name=f"dil_attn{group}",
    )(qv, kvv)
    return o.reshape(b * t, WIDTH), lse.reshape(b * t, WIDTH)


def _block_diag_rows(row, width):
    sel = _head_of_lane((DEC_ROWS, width), 1) == _iota((DEC_ROWS, width), 0)
    return jnp.where(sel, jnp.broadcast_to(row.astype(F32), (DEC_ROWS, width)), 0.0).astype(row.dtype)


def _diag_heads(full):
    sel = _head_of_lane(full.shape, 1) == _iota(full.shape, 0)
    return jnp.sum(jnp.where(sel, full, 0.0), axis=0, keepdims=True)


def _sb_decode_kernel(pt_ref, q_ref, u_ref, cache_ref, o_ref, buf_ref, sem_ref, acc_ref, carry_ref, *, n_pages):
    b = pl.program_id(0)

    def page_copy(p, slot):
        return pltpu.make_async_copy(cache_ref.at[pt_ref[b, p]], buf_ref.at[slot], sem_ref.at[slot])

    page_copy(n_pages - 1, (n_pages - 1) % 2).start()
    if n_pages > 1:
        page_copy(n_pages - 2, (n_pages - 2) % 2).start()
    acc_ref[...] = jnp.zeros_like(acc_ref)
    carry_ref[...] = jnp.zeros_like(carry_ref)
    qbd = _block_diag_rows(q_ref[...], WIDTH)
    u = u_ref[...]

    def cond(c):
        p, mx = c
        return (p >= 0) & (mx >= SB_EXIT)

    def body(c):
        p, _ = c
        slot = lax.rem(p, 2)
        page_copy(p, slot).wait()
        kt = buf_ref[slot, 0].astype(BF16)
        vt = buf_ref[slot, 1].astype(BF16)
        z = _dot(qbd, kt)
        sp = _softplus(z)
        lk = -sp
        hi, lo = _split_bf16(lk)
        tail = _dot(hi, u) + _dot(lo, u)
        w = jnp.exp(z - sp + tail + carry_ref[...])
        acc_ref[...] += _dot_t(w.astype(BF16), vt)
        carry_ref[...] += jnp.sum(lk, axis=1, keepdims=True)
        mx = jnp.max(carry_ref[:N_HEADS, :])

        @pl.when((mx >= SB_EXIT) & (p >= 2))
        def _():
            page_copy(p - 2, slot).start()

        return p - 1, mx

    p_end, _ = lax.while_loop(cond, body, (jnp.int32(n_pages - 1), jnp.float32(0.0)))

    @pl.when(p_end >= 0)
    def _():
        page_copy(p_end, lax.rem(p_end, 2)).wait()

    o_ref[...] = _diag_heads(acc_ref[...])


def _sb_decode(page_table, cache_t, qa_s, umat):
    db, n_pages = page_table.shape
    page_shape = cache_t.shape[1:]
    grid_spec = pltpu.PrefetchScalarGridSpec(
        num_scalar_prefetch=1,
        grid=(db,),
        in_specs=[pl.BlockSpec((None, 1, WIDTH), lambda b, pt: (b, 0, 0)),
                  pl.BlockSpec(umat.shape, lambda b, pt: (0, 0)),
                  pl.BlockSpec(memory_space=pl.ANY)],
        out_specs=pl.BlockSpec((None, 1, WIDTH), lambda b, pt: (b, 0, 0)),
        scratch_shapes=[pltpu.VMEM((2,) + page_shape, F32), pltpu.SemaphoreType.DMA((2,)),
                        pltpu.VMEM((DEC_ROWS, WIDTH), F32), pltpu.VMEM((DEC_ROWS, 1), F32)],
    )
    return pl.pallas_call(
        functools.partial(_sb_decode_kernel, n_pages=n_pages),
        grid_spec=grid_spec,
        out_shape=jax.ShapeDtypeStruct((db, 1, WIDTH), F32),
        compiler_params=_cparams(("arbitrary",), 16 << 20),
        name="sb_decode",
    )(page_table, qa_s.reshape(db, 1, WIDTH), umat, cache_t)


def _mla_decode_kernel(pt_ref, qn_ref, qr_ref, lat_ref, wuk_ref, wuv_ref, gsel_ref, *refs, n_pages):
    page_refs = refs[:n_pages]
    o_ref, ql_ref, qrope_ref, acc_ref, m_ref, l_ref = refs[n_pages:]
    s = pl.program_id(1)

    @pl.when(s == 0)
    def _():
        ql = _dot(_block_diag_rows(qn_ref[...], WIDTH), wuk_ref[...]).astype(BF16)
        shape = (DEC_ROWS, 2 * LANES)
        lane, rowi = _iota(shape, 1), _iota(shape, 0)
        sel = (((lane >= LANES) == (rowi >= 4)) & ((lax.shift_right_logical(lane, 4) & 3) == (rowi & 3))
               & (rowi < N_HEADS))
        qsel = jnp.where(sel, jnp.broadcast_to(qr_ref[...].astype(F32), shape), 0.0).astype(BF16)
        qrope = _dot(qsel, gsel_ref[...]).astype(BF16)
        ql_ref[...] = ql
        qrope_ref[...] = qrope
        own = lat_ref[...].astype(BF16).astype(F32)
        s_own = (jnp.sum(ql.astype(F32) * own[:, :B_KV_RANK], axis=1, keepdims=True)
                 + jnp.sum(qrope.astype(F32) * own[:, B_KV_RANK:], axis=1, keepdims=True)) * MLA_SCALE
        m_ref[...] = s_own
        l_ref[...] = jnp.ones_like(l_ref)
        acc_ref[...] = jnp.broadcast_to(own[:, :B_KV_RANK], acc_ref.shape)

    ql = ql_ref[...]
    qrope = qrope_ref[...]
    for c in range(n_pages // 2):
        lat = jnp.concatenate([page_refs[2 * c][...], page_refs[2 * c + 1][...]], axis=1).astype(BF16)
        ckv = lat[:B_KV_RANK, :]
        sc = (_dot(ql, ckv) + _dot(qrope, lat[B_KV_RANK:, :])) * MLA_SCALE
        m_old = m_ref[...]
        m_new = jnp.maximum(m_old, jnp.max(sc, axis=1, keepdims=True))
        pr = jnp.exp(sc - m_new)
        alpha = jnp.exp(m_old - m_new)
        l_ref[...] = alpha * l_ref[...] + jnp.sum(pr, axis=1, keepdims=True)
        acc_ref[...] = alpha * acc_ref[...] + _dot_t(pr.astype(BF16), ckv)
        m_ref[...] = m_new

    @pl.when(s == pl.num_programs(1) - 1)
    def _():
        o_lat = (acc_ref[...] / l_ref[...]).astype(BF16)
        o_ref[...] = _diag_heads(_dot(o_lat, wuv_ref[...]))


def _mla_decode(page_table, cache_t, qn_s, qr_s, lat_s, wuk_t, wuv, gsel, n_pages):
    db, pages_per_seq = page_table.shape
    n_steps = pages_per_seq // n_pages
    page_rows = cache_t.shape[2]

    def page_spec(k):
        return pl.BlockSpec((None, B_LATENT, page_rows), lambda b, s, pt: (pt[b, s * n_pages + k], 0, 0))

    per_b = lambda n: pl.BlockSpec((None, 1, n), lambda b, s, pt: (b, 0, 0))
    full = lambda a: pl.BlockSpec(a.shape, lambda b, s, pt: (0,) * a.ndim)
    grid_spec = pltpu.PrefetchScalarGridSpec(
        num_scalar_prefetch=1,
        grid=(db, n_steps),
        in_specs=[per_b(WIDTH), per_b(2 * LANES), per_b(B_LATENT), full(wuk_t), full(wuv), full(gsel)]
                 + [page_spec(k) for k in range(n_pages)],
        out_specs=per_b(WIDTH),
        scratch_shapes=[pltpu.VMEM((DEC_ROWS, B_KV_RANK), BF16), pltpu.VMEM((DEC_ROWS, B_ROPE), BF16),
                        pltpu.VMEM((DEC_ROWS, B_KV_RANK), F32), pltpu.VMEM((DEC_ROWS, 1), F32),
                        pltpu.VMEM((DEC_ROWS, 1), F32)],
    )
    return pl.pallas_call(
        functools.partial(_mla_decode_kernel, n_pages=n_pages),
        grid_spec=grid_spec,
        out_shape=jax.ShapeDtypeStruct((db, 1, WIDTH), F32),
        compiler_params=_cparams(("parallel", "arbitrary"), 40 << 20),
        name="mla_decode",
    )(page_table, qn_s.reshape(db, 1, WIDTH), qr_s.reshape(db, 1, 2 * LANES), lat_s.reshape(db, 1, B_LATENT),
      wuk_t, wuv, gsel, *([cache_t] * n_pages))


def _dil_decode_kernel(q_ref, kvn_ref, c0_ref, c1_ref, c2_ref, o_ref):
    caches = (c0_ref, c1_ref, c2_ref)
    head = _iota((DEC_ROWS, 1), 0).astype(F32)
    outs, lses = [], []
    for g in range(C_GROUPS):
        r = C_DILATIONS[g]
        win = caches[g].shape[-1]
        slope = jnp.exp((-8.0 * (g * N_HEADS + head + 1.0) / (C_GROUPS * N_HEADS)) * np.log(2.0))
        qbd = _block_diag_rows(q_ref[:, g * WIDTH:(g + 1) * WIDTH], WIDTH)
        kt = caches[g][0].astype(BF16)
        vt = caches[g][1].astype(BF16)
        k_new = kvn_ref[:, g * 2 * WIDTH:g * 2 * WIDTH + WIDTH].astype(BF16)
        v_new = kvn_ref[:, g * 2 * WIDTH + WIDTH:(g + 1) * 2 * WIDTH].astype(BF16).astype(F32)
        row = _iota((DEC_ROWS, win), 1)
        s_win = _dot(qbd, kt) - slope * (win - row).astype(F32)
        s_win = jnp.where((row & (r - 1)) == 0, s_win, NEG_INF)
        s_own = jnp.sum(qbd.astype(F32) * k_new.astype(F32), axis=1, keepdims=True)
        mx = jnp.maximum(jnp.max(s_win, axis=1, keepdims=True), s_own)
        p_win = jnp.exp(s_win - mx)
        p_own = jnp.exp(s_own - mx)
        den = jnp.sum(p_win, axis=1, keepdims=True) + p_own
        o = (_dot_t(p_win.astype(BF16), vt) + p_own.astype(BF16).astype(F32) * v_new) / den
        outs.append(o)
        lses.append(mx + jnp.log(den))
    mx = functools.reduce(jnp.maximum, lses)
    es = [jnp.exp(l - mx) for l in lses]
    tot = functools.reduce(lambda a, c: a + c, es)
    merged = functools.reduce(lambda a, c: a + c, [(e / tot) * o for e, o in zip(es, outs)])
    o_ref[...] = _diag_heads(merged)


def _dil_decode(q3, kvn3, caches_t):
    db = q3.shape[0]
    per_b = lambda n: pl.BlockSpec((None, 1, n), lambda b: (b, 0, 0))
    specs = []
    for g, c in enumerate(caches_t):
        assert c.shape[-1] == C_STEPS * C_DILATIONS[g]
        specs.append(pl.BlockSpec((None,) + c.shape[1:], lambda b: (b, 0, 0, 0)))
    return pl.pallas_call(
        _dil_decode_kernel,
        grid=(db,),
        in_specs=[per_b(C_GROUPS * WIDTH), per_b(C_GROUPS * 2 * WIDTH)] + specs,
        out_specs=per_b(WIDTH),
        out_shape=jax.ShapeDtypeStruct((db, 1, WIDTH), F32),
        compiler_params=_cparams(("parallel",), 48 << 20),
        name="dil_decode",
    )(q3.reshape(db, 1, C_GROUPS * WIDTH), kvn3.reshape(db, 1, C_GROUPS * 2 * WIDTH), *caches_t)


def _prep_w_in0(w_in0):
    cuts = np.cumsum(IN0_SIZES)[:-1].tolist()
    qa, ka, va, dq, dkv, kr, gate = jnp.split(w_in0, cuts, axis=1)
    kr_sw = jnp.concatenate([kr[:, B_HALF:], kr[:, :B_HALF]], axis=1)
    wq = jnp.concatenate([qa, gate, dq], axis=1).astype(BF16)
    wk = jnp.concatenate([ka, va, dkv, kr, kr_sw], axis=1).T.astype(BF16)
    return wq, wk


def _prep_w_uq(w_uq):
    w = w_uq.reshape(B_Q_RANK, N_HEADS, B_NOPE + B_ROPE)
    nope = w[:, :, :B_NOPE].reshape(B_Q_RANK, WIDTH)
    x1 = w[:, :, B_NOPE:B_NOPE + B_HALF]
    x2 = w[:, :, B_NOPE + B_HALF:]
    slabs, slabs_sw = [], []
    for s in range(2):
        a = x1[:, 4 * s:4 * s + 4].reshape(B_Q_RANK, 4 * B_HALF)
        c = x2[:, 4 * s:4 * s + 4].reshape(B_Q_RANK, 4 * B_HALF)
        slabs.append(jnp.concatenate([a, c], axis=1))
        slabs_sw.append(jnp.concatenate([c, a], axis=1))
    return jnp.concatenate([nope] + slabs + slabs_sw, axis=1).astype(BF16)


def _rope_tables(pos):
    inv = ROPE_THETA ** (-jnp.arange(0, B_ROPE, 2, dtype=F32) / B_ROPE)
    ang = pos.astype(F32)[:, None] * inv[None, :]
    cos, sin = jnp.cos(ang), jnp.sin(ang)
    tabq = jnp.concatenate([jnp.tile(cos, (1, 8)), jnp.tile(-sin, (1, 4)), jnp.tile(sin, (1, 4))], axis=1)
    tabk = jnp.concatenate([cos, cos, -sin, sin], axis=1).T
    return tabq, tabk


def _const_mats():
    tri = lambda n: jnp.asarray((np.arange(n)[:, None] > np.arange(n)[None, :]).astype(np.float32), BF16)
    emat = np.zeros((B_ROPE, LANES), np.float32)
    for c in range(LANES):
        emat[(c % B_HALF) + (B_HALF if c >= 4 * B_HALF else 0), c] = 1.0
    gsel = np.zeros((2 * LANES, B_ROPE), np.float32)
    for c in range(2 * LANES):
        cc = c % LANES
        gsel[c, (cc % B_HALF) + (B_HALF if cc >= 4 * B_HALF else 0)] = 1.0
    return tri(MXU_TILE), tri(LANES), jnp.asarray(emat.T, BF16), jnp.asarray(gsel, BF16)


def _kv_rows(a_t, n, t):
    return jnp.transpose(a_t.reshape(n, 2, N_HEADS, HEAD_DIM, t), (0, 4, 1, 2, 3))


def _kv_pages(c):
    return jnp.transpose(c, (0, 2, 3, 4, 1)).reshape(c.shape[0], 2, WIDTH, c.shape[1])


def kernel(x_prompt, x_sample, cache_a_kv, cache_b_latent, cache_c0_kv, cache_c1_kv, cache_c2_kv,
           page_table, w_in0, g_cq, g_ckv, w_uq, w_uk, w_uv, w_out0, ln0_g, ln0_b,
           w_in1, w_out1, ln1_g, ln1_b):
    bsz, t_p, _ = x_prompt.shape
    db, t_s, _ = x_sample.shape
    assert t_s == 1 and t_p % MXU_TILE == 0 and cache_a_kv.shape[1] == LANES
    past_len = page_table.shape[1] * cache_a_kv.shape[1]
    m_p, m_s = bsz * t_p, db * t_s
    tm_p, tm_s = MXU_TILE, m_s

    tri_blk, tri_page, emat_t, gsel = _const_mats()
    wq0, wk0 = _prep_w_in0(w_in0)
    wuq = _prep_w_uq(w_uq)
    wuk_b, wuv_b = w_uk.astype(BF16), w_uv.astype(BF16)
    wo0, wo1, w1 = w_out0.astype(BF16), w_out1.astype(BF16), w_in1.astype(BF16)
    row = lambda v: v.reshape(1, -1)
    tabq_p, tabk_p = _rope_tables(jnp.arange(t_p))
    tabq_s, tabk_s = _rope_tables(jnp.full((m_s,), past_len, jnp.int32))

    xp = x_prompt.reshape(m_p, D_MODEL)
    xs3 = x_sample.reshape(1, m_s, D_MODEL)
    xs = xs3.reshape(m_s, D_MODEL)

    proj = lambda x3, tq_, tk_, tm: _l0_proj(x3, tq_, tk_, wq0, wk0, row(g_cq), g_ckv.reshape(-1, 1), wuq,
                                             wuk_b.T, wuv_b.T, emat_t, tm)
    qa_p, gate_p, qn_p, qr_p, kvf_p, lat_p, kvb_p, kcat_p, vb_p = proj(x_prompt, tabq_p, tabk_p, tm_p)
    qa_s, gate_s, qn_s, qr_s, kvf_s, lat_s, _, _, _ = proj(xs3, tabq_s, tabk_s, tm_s)

    oa_p = _sb_attn(qa_p, kvb_p, tri_blk).reshape(m_p, WIDTH)
    ob_p = _mla_attn(qn_p, qr_p, kcat_p, vb_p).reshape(m_p, WIDTH)
    h_p = _finish0(xp, oa_p, ob_p, gate_p.reshape(m_p, 2 * WIDTH), wo0, row(ln0_g), row(ln0_b), tm_p)

    lat_s_rows = lat_s[0].T
    oa_s = _sb_decode(page_table, _kv_pages(cache_a_kv), qa_s[0], tri_page).reshape(m_s, WIDTH)
    ob_s = _mla_decode(page_table, jnp.transpose(cache_b_latent, (0, 2, 1)), qn_s[0], qr_s[0], lat_s_rows,
                       wuk_b.T, wuv_b, gsel, n_pages=16).reshape(m_s, WIDTH)
    h_s = _finish0(xs, oa_s, ob_s, gate_s[0], wo0, row(ln0_g), row(ln0_b), tm_s)

    outs_p = _l1_proj(h_p, w1, tm_p, BF16)
    q_p, kvb1_p, gate1_p = outs_p[0:3], outs_p[3:6], outs_p[6]
    r3 = lambda a: a.reshape(bsz, t_p, a.shape[-1])
    os_p, lses_p, c_new_p = [], [], []
    for g, c in enumerate((cache_c0_kv, cache_c1_kv, cache_c2_kv)):
        o, lse = _dil_attn(r3(q_p[g]), r3(kvb1_p[g]), g)
        os_p.append(o)
        lses_p.append(lse)
        win = min(c.shape[1], t_p)
        w_kv_t = w1[:, g * 3 * WIDTH + WIDTH:(g + 1) * 3 * WIDTH].T
        c_new_p.append(_kv_rows(_proj_t(r3(h_p), w_kv_t, win, min(win, MXU_TILE)), bsz, win))
    y_p = _finish1(h_p, os_p, lses_p, gate1_p, wo1, row(ln1_g), row(ln1_b), tm_p)

    outs_s = _l1_proj(h_s, w1, tm_s, F32)
    q_s, kvf1_s, gate1_s = outs_s[0:3], outs_s[3:6], outs_s[6]
    oc_s = _dil_decode(jnp.concatenate(q_s, axis=1), jnp.concatenate(kvf1_s, axis=1),
                       [_kv_pages(c) for c in (cache_c0_kv, cache_c1_kv, cache_c2_kv)]).reshape(m_s, WIDTH)
    y_s = _finish1(h_s, [oc_s], [], gate1_s, wo1, row(ln1_g), row(ln1_b), tm_s)

    kv5 = lambda a: a.reshape(db, t_s, 2, N_HEADS, HEAD_DIM)
    return (y_p.reshape(bsz, t_p, D_MODEL), y_s.reshape(db, t_s, D_MODEL),
            _kv_rows(kvf_p, bsz, t_p), kv5(kvf_s[0].T),
            jnp.transpose(lat_p, (0, 2, 1)), lat_s_rows.reshape(db, t_s, B_LATENT),
            c_new_p[0], kv5(kvf1_s[0]), c_new_p[1], kv5(kvf1_s[1]), c_new_p[2], kv5(kvf1_s[2]))
```

```python
import functools

import numpy as np
import jax
import jax.numpy as jnp
from jax import lax
from jax.experimental import pallas as pl
from jax.experimental.pallas import tpu as pltpu

D_MODEL = 1024
DEPTH = 2
HEAD_DIM = 64
N_HEADS = 8
WIDTH = N_HEADS * HEAD_DIM
ATTN_SCALE = HEAD_DIM ** -0.5
B_NOPE = 64
B_ROPE = 32
B_HALF = B_ROPE // 2
B_Q_RANK = 384
B_KV_RANK = 256
B_LATENT = B_KV_RANK + B_ROPE
MLA_SCALE = (B_NOPE + B_ROPE) ** -0.5
ROPE_THETA = 10000.0
C_DILATIONS = (1, 4, 16)
C_GROUPS = 3
C_STEPS = 128
NORM_EPS = 1e-5
DN_ALPHA = (2 * DEPTH) ** 0.25
IN0_SIZES = (WIDTH, WIDTH, WIDTH, B_Q_RANK, B_KV_RANK, B_ROPE, 2 * WIDTH)

LANES = 128
PAIR = 2 * HEAD_DIM
N_PAIRS = N_HEADS // 2
V7X_VMEM_BYTES = 64 * 1024 * 1024
MXU_TILE = 256
DEC_ROWS = 16

NEG_INF = float("-inf")
SB_EXIT = -104.0
F32 = jnp.float32
BF16 = jnp.bfloat16


def _cparams(semantics, vmem_bytes):
    return pltpu.CompilerParams(dimension_semantics=semantics,
                                vmem_limit_bytes=min(int(vmem_bytes), V7X_VMEM_BYTES - (4 << 20)))


def _dot(a, b):
    return jnp.dot(a, b, preferred_element_type=F32)


def _dot_t(a, b):
    return lax.dot_general(a, b, (((1,), (1,)), ((), ())), preferred_element_type=F32)


def _iota(shape, dim):
    return lax.broadcasted_iota(jnp.int32, shape, dim)


def _softplus(z):
    return jnp.maximum(z, 0.0) + jnp.log(1.0 + jnp.exp(-jnp.abs(z)))


def _split_bf16(x):
    hi = x.astype(BF16)
    lo = (x - hi.astype(F32)).astype(BF16)
    return hi, lo


def _pair_half_mask(rows, head):
    lane = _iota((rows, PAIR), 1)
    return (lane >= HEAD_DIM) if head % 2 else (lane < HEAD_DIM)


def _keep(mask, x):
    return jnp.where(mask, x.astype(F32), 0.0).astype(x.dtype)


def _head_of_lane(shape, dim):
    return lax.shift_right_logical(_iota(shape, dim), 6)


def _twice(x):
    return jnp.concatenate([x, x], axis=1)


_Q_QA, _Q_GATE, _Q_DQ, _N_Q0 = 0, WIDTH, 3 * WIDTH, 3 * WIDTH + B_Q_RANK
_K_KV, _K_DKV, _K_KR, _N_K0 = 0, 2 * WIDTH, 2 * WIDTH + B_KV_RANK, 2 * WIDTH + B_KV_RANK + 2 * B_ROPE
_N_UQ = WIDTH + 4 * LANES


def _l0_proj_kernel(x_ref, tabq_ref, tabk_ref, wq_ref, wk_ref, gcq_ref, gckv_ref, wuq_ref, wuk_ref, wuv_ref,
                    e_ref, qa_ref, gate_ref, qn_ref, qr_ref, kvf_ref, lat_ref, kvb_ref, kcat_ref, vb_ref):
    xb = x_ref[...].astype(BF16)
    qa_ref[...] = (_dot(xb, wq_ref[:, _Q_QA:_Q_GATE]) * ATTN_SCALE).astype(BF16)
    gate_ref[...] = _dot(xb, wq_ref[:, _Q_GATE:_Q_DQ])
    dq = _dot(xb, wq_ref[:, _Q_DQ:_N_Q0])
    dqn = dq * lax.rsqrt(jnp.mean(dq * dq, axis=-1, keepdims=True) + NORM_EPS) * gcq_ref[...]
    qf = _dot(dqn.astype(BF16), wuq_ref[...])
    qn_ref[...] = qf[:, :WIDTH].astype(BF16)
    cosq = tabq_ref[:, :LANES]
    sinq = tabq_ref[:, LANES:]
    for s in range(2):
        nat = qf[:, WIDTH + s * LANES:WIDTH + (s + 1) * LANES]
        swp = qf[:, WIDTH + (2 + s) * LANES:WIDTH + (3 + s) * LANES]
        qr_ref[:, s * LANES:(s + 1) * LANES] = (nat * cosq + swp * sinq).astype(BF16)
    kv = _dot_t(wk_ref[_K_KV:_K_DKV, :], xb)
    kvf_ref[...] = kv
    kvb_ref[...] = kv.astype(BF16)
    dkv = _dot_t(wk_ref[_K_DKV:_K_KR, :], xb)
    ckv = dkv * lax.rsqrt(jnp.mean(dkv * dkv, axis=0, keepdims=True) + NORM_EPS) * gckv_ref[...]
    lat_ref[:B_KV_RANK, :] = ckv
    kr2 = _dot_t(wk_ref[_K_KR:_N_K0, :], xb)
    krot = kr2[:B_ROPE, :] * tabk_ref[:B_ROPE, :] + kr2[B_ROPE:, :] * tabk_ref[B_ROPE:, :]
    lat_ref[B_KV_RANK:, :] = krot
    ckv_b = ckv.astype(BF16)
    kn = _dot(wuk_ref[...], ckv_b).astype(BF16)
    vb_ref[...] = _dot(wuv_ref[...], ckv_b).astype(BF16)
    kslab = _dot(e_ref[...], krot.astype(BF16)).astype(BF16)
    for p in range(N_PAIRS):
        kcat_ref[p * MXU_TILE:p * MXU_TILE + PAIR, :] = kn[p * PAIR:(p + 1) * PAIR, :]
        kcat_ref[p * MXU_TILE + PAIR:(p + 1) * MXU_TILE, :] = kslab


def _l0_proj(x3, tabq, tabk, wq, wk, gcq, gckv, wuq, wuk_t, wuv_t, emat_t, tm):
    b, t, _ = x3.shape
    nt = t // tm
    tok = lambda n: pl.BlockSpec((None, tm, n), lambda bi, ti: (bi, ti, 0))
    feat = lambda n: pl.BlockSpec((None, n, tm), lambda bi, ti: (bi, 0, ti))
    featb = lambda n: pl.BlockSpec((None, None, n, tm), lambda bi, ti: (bi, ti, 0, 0))
    full = lambda a: pl.BlockSpec(a.shape, lambda bi, ti: (0,) * a.ndim, pipeline_mode=pl.Buffered(1))
    s = jax.ShapeDtypeStruct
    return pl.pallas_call(
        _l0_proj_kernel,
        grid=(b, nt),
        in_specs=[tok(D_MODEL), pl.BlockSpec((tm, 2 * LANES), lambda bi, ti: (ti, 0)),
                  pl.BlockSpec((2 * B_ROPE, tm), lambda bi, ti: (0, ti)),
                  full(wq), full(wk), full(gcq), full(gckv), full(wuq), full(wuk_t), full(wuv_t), full(emat_t)],
        out_specs=[tok(WIDTH), tok(2 * WIDTH), tok(WIDTH), tok(2 * LANES),
                   feat(2 * WIDTH), feat(B_LATENT),
                   featb(2 * WIDTH), featb(N_PAIRS * MXU_TILE), featb(WIDTH)],
        out_shape=[s((b, t, WIDTH), BF16), s((b, t, 2 * WIDTH), F32), s((b, t, WIDTH), BF16),
                   s((b, t, 2 * LANES), BF16),
                   s((b, 2 * WIDTH, t), F32), s((b, B_LATENT, t), F32),
                   s((b, nt, 2 * WIDTH, tm), BF16), s((b, nt, N_PAIRS * MXU_TILE, tm), BF16),
                   s((b, nt, WIDTH, tm), BF16)],
        compiler_params=_cparams(("parallel", "parallel"), 48 << 20),
        name="l0_proj",
    )(x3, tabq, tabk, wq, wk, gcq, gckv, wuq, wuk_t, wuv_t, emat_t)


def _sb_attn_kernel(q_ref, kv_ref, u_ref, o_ref, qp_ref, acc_ref, carry_ref, *, tq):
    i = pl.program_id(1)
    acc_ref[...] = jnp.zeros_like(acc_ref)
    carry_ref[...] = jnp.zeros_like(carry_ref)
    for h in range(N_HEADS):
        qp_ref[h] = _keep(_pair_half_mask(tq, h), q_ref[:, (h // 2) * PAIR:(h // 2 + 1) * PAIR])

    def block(j, masked):
        u = u_ref[...]
        if masked:
            strict = _iota((tq, tq), 1) < _iota((tq, tq), 0)
        for h in range(N_HEADS):
            p = h // 2
            kt = kv_ref[j, p * PAIR:(p + 1) * PAIR, :]
            vt = kv_ref[j, WIDTH + p * PAIR:WIDTH + (p + 1) * PAIR, :]
            z = _dot(qp_ref[h], kt)
            sp = _softplus(z)
            lk = jnp.where(strict, -sp, 0.0) if masked else -sp
            hi, lo = _split_bf16(lk)
            tail = _dot(hi, u) + _dot(lo, u)
            w = jnp.exp(z - sp + tail + _twice(carry_ref[h]))
            if masked:
                w = jnp.where(strict, w, 0.0)
            acc_ref[h] += _dot_t(w.astype(BF16), vt)
            carry_ref[h] += jnp.sum(lk, axis=1, keepdims=True)

    def least_decayed():
        m = carry_ref[0]
        for h in range(1, N_HEADS):
            m = jnp.maximum(m, carry_ref[h])
        return jnp.max(m)

    block(i, True)

    def cond(c):
        jj, mx = c
        return (jj <= i) & (mx >= SB_EXIT)

    def body(c):
        jj, _ = c
        block(i - jj, False)
        return jj + 1, least_decayed()

    lax.while_loop(cond, body, (jnp.int32(1), least_decayed()))
    for p in range(N_PAIRS):
        o_ref[:, p * PAIR:(p + 1) * PAIR] = jnp.where(_pair_half_mask(tq, 0), acc_ref[2 * p], acc_ref[2 * p + 1])


def _sb_attn(qa, kvb, umat):
    b, t, _ = qa.shape
    _, nblk, _, tq = kvb.shape
    return pl.pallas_call(
        functools.partial(_sb_attn_kernel, tq=tq),
        grid=(b, nblk),
        in_specs=[pl.BlockSpec((None, tq, WIDTH), lambda bi, i: (bi, i, 0)),
                  pl.BlockSpec((None, nblk, 2 * WIDTH, tq), lambda bi, i: (bi, 0, 0, 0)),
                  pl.BlockSpec(umat.shape, lambda bi, i: (0, 0))],
        out_specs=pl.BlockSpec((None, tq, WIDTH), lambda bi, i: (bi, i, 0)),
        out_shape=jax.ShapeDtypeStruct((b, t, WIDTH), F32),
        scratch_shapes=[pltpu.VMEM((N_HEADS, tq, PAIR), BF16), pltpu.VMEM((N_HEADS, tq, PAIR), F32),
                        pltpu.VMEM((N_HEADS, tq, PAIR), F32)],
        compiler_params=_cparams(("parallel", "arbitrary"), 40 << 20),
        name="sb_attn",
    )(qa, kvb, umat)


def _rope_slab_mask(rows, head):
    lane = _iota((rows, LANES), 1)
    return (lax.shift_right_logical(lane, 4) & 3) == (head % 4)


def _mla_attn_kernel(qn_ref, qr_ref, kcat_ref, vb_ref, o_ref, qc_ref, acc_ref, m_ref, l_ref, *, tq):
    i = pl.program_id(1)
    acc_ref[...] = jnp.zeros_like(acc_ref)
    m_ref[...] = jnp.full_like(m_ref, NEG_INF)
    l_ref[...] = jnp.zeros_like(l_ref)
    for h in range(N_HEADS):
        qc_ref[h, :, :PAIR] = _keep(_pair_half_mask(tq, h), qn_ref[:, (h // 2) * PAIR:(h // 2 + 1) * PAIR])
        qc_ref[h, :, PAIR:] = _keep(_rope_slab_mask(tq, h), qr_ref[:, (h // 4) * LANES:(h // 4 + 1) * LANES])

    def block(j, masked):
        if masked:
            causal = _iota((tq, tq), 1) <= _iota((tq, tq), 0)
        for h in range(N_HEADS):
            p = h // 2
            kt = kcat_ref[j, p * MXU_TILE:(p + 1) * MXU_TILE, :]
            vt = vb_ref[j, p * PAIR:(p + 1) * PAIR, :]
            s = _dot(qc_ref[h], kt) * MLA_SCALE
            if masked:
                s = jnp.where(causal, s, NEG_INF)
            m_old = m_ref[h]
            m_new = jnp.maximum(m_old, jnp.max(s, axis=1, keepdims=True))
            pr = jnp.exp(s - _twice(m_new))
            alpha = jnp.exp(m_old - m_new)
            l_ref[h] = alpha * l_ref[h] + jnp.sum(pr, axis=1, keepdims=True)
            acc_ref[h] = alpha * acc_ref[h] + _dot_t(pr.astype(BF16), vt)
            m_ref[h] = m_new

    block(i, True)

    def body(j, c):
        block(j, False)
        return c

    lax.fori_loop(0, i, body, 0)
    for p in range(N_PAIRS):
        o_ref[:, p * PAIR:(p + 1) * PAIR] = jnp.where(
            _pair_half_mask(tq, 0), acc_ref[2 * p] / l_ref[2 * p], acc_ref[2 * p + 1] / l_ref[2 * p + 1])


def _mla_attn(qn, qr, kcat, vb):
    b, t, _ = qn.shape
    _, nblk, _, tq = kcat.shape
    return pl.pallas_call(
        functools.partial(_mla_attn_kernel, tq=tq),
        grid=(b, nblk),
        in_specs=[pl.BlockSpec((None, tq, WIDTH), lambda bi, i: (bi, i, 0)),
                  pl.BlockSpec((None, tq, 2 * LANES), lambda bi, i: (bi, i, 0)),
                  pl.BlockSpec((None, nblk, N_PAIRS * MXU_TILE, tq), lambda bi, i: (bi, 0, 0, 0)),
                  pl.BlockSpec((None, nblk, WIDTH, tq), lambda bi, i: (bi, 0, 0, 0))],
        out_specs=pl.BlockSpec((None, tq, WIDTH), lambda bi, i: (bi, i, 0)),
        out_shape=jax.ShapeDtypeStruct((b, t, WIDTH), F32),
        scratch_shapes=[pltpu.VMEM((N_HEADS, tq, MXU_TILE), BF16), pltpu.VMEM((N_HEADS, tq, PAIR), F32),
                        pltpu.VMEM((N_HEADS, tq, PAIR), F32), pltpu.VMEM((N_HEADS, tq, PAIR), F32)],
        compiler_params=_cparams(("parallel", "arbitrary"), 40 << 20),
        name="mla_attn",
    )(qn, qr, kcat, vb)


def _layer_norm(y, g, b):
    mu = jnp.mean(y, axis=-1, keepdims=True)
    d = y - mu
    var = jnp.mean(d * d, axis=-1, keepdims=True)
    return d * lax.rsqrt(var + NORM_EPS) * g + b


def _silu(g):
    return g / (1.0 + jnp.exp(-g))


def _finish0_kernel(x_ref, oa_ref, ob_ref, gate_ref, w_ref, g_ref, b_ref, h_ref):
    sg = _silu(gate_ref[...])
    y = (_dot((oa_ref[...] * sg[:, :WIDTH]).astype(BF16), w_ref[:WIDTH, :])
         + _dot((ob_ref[...] * sg[:, WIDTH:]).astype(BF16), w_ref[WIDTH:, :]))
    h_ref[...] = _layer_norm(DN_ALPHA * x_ref[...] + y, g_ref[...], b_ref[...])


def _finish0(x2, oa, ob, gate, w, g, b, tm):
    m = x2.shape[0]
    row = lambda n: pl.BlockSpec((tm, n), lambda i: (i, 0))
    full = lambda a: pl.BlockSpec(a.shape, lambda i: (0,) * a.ndim)
    return pl.pallas_call(
        _finish0_kernel,
        grid=(m // tm,),
        in_specs=[row(D_MODEL), row(WIDTH), row(WIDTH), row(2 * WIDTH), full(w), full(g), full(b)],
        out_specs=row(D_MODEL),
        out_shape=jax.ShapeDtypeStruct((m, D_MODEL), F32),
        compiler_params=_cparams(("parallel",), 40 << 20),
        name="finish0",
    )(x2, oa, ob, gate, w, g, b)


def _finish1_kernel(*refs, n_groups):
    x_ref = refs[0]
    o_refs = refs[1:1 + n_groups]
    l_refs = refs[1 + n_groups:1 + 2 * n_groups] if n_groups > 1 else ()
    gate_ref, w_ref, g_ref, b_ref, h_ref = refs[-5:]
    if n_groups > 1:
        lses = [r[...] for r in l_refs]
        mx = functools.reduce(jnp.maximum, lses)
        es = [jnp.exp(l - mx) for l in lses]
        num = functools.reduce(lambda a, c: a + c, [e * r[...] for e, r in zip(es, o_refs)])
        o = num / functools.reduce(lambda a, c: a + c, es)
    else:
        o = o_refs[0][...]
    y = _dot((o * _silu(gate_ref[...])).astype(BF16), w_ref[...])
    h_ref[...] = _layer_norm(DN_ALPHA * x_ref[...] + y, g_ref[...], b_ref[...])


def _finish1_streams_kernel(x_ref, *refs, tm):
    o_refs, l_refs = refs[0:C_GROUPS], refs[C_GROUPS:2 * C_GROUPS]
    gate_ref, w_ref, g_ref, b_ref, h_ref, tok_ref = refs[2 * C_GROUPS:]
    n_slabs = WIDTH // LANES

    def to_tokens(ref, r, slot):
        if r == 1:
            return ref[0]
        for rho in range(r):
            for c in range(n_slabs):
                tok_ref[slot, c, pl.ds(rho, tm // r, stride=r), :] = ref[rho, :, c * LANES:(c + 1) * LANES]
        return jnp.concatenate([tok_ref[slot, c] for c in range(n_slabs)], axis=1)

    lses = [to_tokens(l_refs[g], r, 2 * g) for g, r in enumerate(C_DILATIONS)]
    mx = functools.reduce(jnp.maximum, lses)
    es = [jnp.exp(l - mx) for l in lses]
    den = functools.reduce(lambda a, c: a + c, es)
    num = functools.reduce(lambda a, c: a + c,
                           [e * to_tokens(o_refs[g], r, 2 * g + 1) for g, (e, r) in enumerate(zip(es, C_DILATIONS))])
    y = _dot((num / den * _silu(gate_ref[...])).astype(BF16), w_ref[...])
    h_ref[...] = _layer_norm(DN_ALPHA * x_ref[...] + y, g_ref[...], b_ref[...])


def _finish1_streams(x3, os_, lses, gate3, w, g, b, tm):
    bsz, t, _ = x3.shape
    tok = lambda n: pl.BlockSpec((None, tm, n), lambda bi, ti: (bi, ti, 0))
    stream = lambda r: pl.BlockSpec((None, r, tm // r, WIDTH), lambda bi, ti: (bi, 0, ti, 0))
    full = lambda a: pl.BlockSpec(a.shape, lambda bi, ti: (0,) * a.ndim)
    return pl.pallas_call(
        functools.partial(_finish1_streams_kernel, tm=tm),
        grid=(bsz, t // tm),
        in_specs=([tok(D_MODEL)] + [stream(r) for r in C_DILATIONS] * 2 + [tok(WIDTH), full(w), full(g), full(b)]),
        out_specs=tok(D_MODEL),
        out_shape=jax.ShapeDtypeStruct((bsz, t, D_MODEL), F32),
        scratch_shapes=[pltpu.VMEM((2 * C_GROUPS, WIDTH // LANES, tm, LANES), F32)],
        compiler_params=_cparams(("parallel", "parallel"), 40 << 20),
        name="finish1_streams",
    )(x3, *os_, *lses, gate3, w, g, b)


def _finish1(x2, os_, lses, gate, w, g, b, tm):
    m = x2.shape[0]
    n_groups = len(os_)
    row = lambda n: pl.BlockSpec((tm, n), lambda i: (i, 0))
    full = lambda a: pl.BlockSpec(a.shape, lambda i: (0,) * a.ndim)
    args = [x2, *os_, *lses, gate, w, g, b]
    in_specs = ([row(D_MODEL)] + [row(WIDTH)] * (len(os_) + len(lses)) + [row(WIDTH), full(w), full(g), full(b)])
    return pl.pallas_call(
        functools.partial(_finish1_kernel, n_groups=n_groups),
        grid=(m // tm,),
        in_specs=in_specs,
        out_specs=row(D_MODEL),
        out_shape=jax.ShapeDtypeStruct((m, D_MODEL), F32),
        compiler_params=_cparams(("parallel",), 40 << 20),
        name="finish1",
    )(*args)


def _l1_proj_kernel(x_ref, w_ref, *out_refs, kv_dtype):
    xb = x_ref[...].astype(BF16)
    q_refs = out_refs[0:C_GROUPS]
    kv_refs = out_refs[C_GROUPS:2 * C_GROUPS]
    gate_ref = out_refs[2 * C_GROUPS]
    for g in range(C_GROUPS):
        base = g * 3 * WIDTH
        q_refs[g][...] = (_dot(xb, w_ref[:, base:base + WIDTH]) * ATTN_SCALE).astype(BF16)
        kv_refs[g][...] = _dot(xb, w_ref[:, base + WIDTH:base + 3 * WIDTH]).astype(kv_dtype)
    gate_ref[...] = _dot(xb, w_ref[:, C_GROUPS * 3 * WIDTH:])


def _l1_proj(x2, w1, tm, kv_dtype):
    m = x2.shape[0]
    row = lambda n: pl.BlockSpec((tm, n), lambda i: (i, 0))
    outs = [(WIDTH, BF16)] * C_GROUPS + [(2 * WIDTH, kv_dtype)] * C_GROUPS + [(WIDTH, F32)]
    return pl.pallas_call(
        functools.partial(_l1_proj_kernel, kv_dtype=kv_dtype),
        grid=(m // tm,),
        in_specs=[row(D_MODEL),
                  pl.BlockSpec(w1.shape, lambda i: (0, 0), pipeline_mode=pl.Buffered(1))],
        out_specs=[row(n) for n, _ in outs],
        out_shape=[jax.ShapeDtypeStruct((m, n), dt) for n, dt in outs],
        compiler_params=_cparams(("parallel",), 48 << 20),
        name="l1_proj",
    )(x2, w1)


def _stream_rows(slab_ref, r, tm):
    n_slabs = slab_ref.shape[0]
    return jnp.concatenate(
        [jnp.concatenate([slab_ref[c, pl.ds(rho, tm // r, stride=r), :] for c in range(n_slabs)], axis=1)
         for rho in range(r)], axis=0)


def _l1_proj_streams_kernel(x_ref, w_ref, *refs, tm):
    q_refs, kv_refs = refs[0:C_GROUPS], refs[C_GROUPS:2 * C_GROUPS]
    gate_ref, xs_ref = refs[2 * C_GROUPS], refs[2 * C_GROUPS + 1]
    for c in range(D_MODEL // LANES):
        xs_ref[c] = x_ref[:, c * LANES:(c + 1) * LANES]
    xb = x_ref[...].astype(BF16)
    gate_ref[...] = _dot(xb, w_ref[:, C_GROUPS * 3 * WIDTH:])
    for g in range(C_GROUPS):
        r = C_DILATIONS[g]
        rows = tm // r
        xg = xb if r == 1 else _stream_rows(xs_ref, r, tm).astype(BF16)
        base = g * 3 * WIDTH
        q = (_dot(xg, w_ref[:, base:base + WIDTH]) * ATTN_SCALE).astype(BF16)
        kv = _dot(xg, w_ref[:, base + WIDTH:base + 3 * WIDTH]).astype(BF16)
        for rho in range(r):
            q_refs[g][rho] = q[rho * rows:(rho + 1) * rows]
            kv_refs[g][rho] = kv[rho * rows:(rho + 1) * rows]


def _l1_proj_streams(x3, w1, tm):
    b, t, _ = x3.shape
    s = jax.ShapeDtypeStruct
    stream = lambda r, n: pl.BlockSpec((None, r, tm // r, n), lambda bi, ti: (bi, 0, ti, 0))
    out_specs = ([stream(r, WIDTH) for r in C_DILATIONS] + [stream(r, 2 * WIDTH) for r in C_DILATIONS]
                 + [pl.BlockSpec((None, tm, WIDTH), lambda bi, ti: (bi, ti, 0))])
    out_shape = ([s((b, r, t // r, WIDTH), BF16) for r in C_DILATIONS]
                 + [s((b, r, t // r, 2 * WIDTH), BF16) for r in C_DILATIONS] + [s((b, t, WIDTH), F32)])
    return pl.pallas_call(
        functools.partial(_l1_proj_streams_kernel, tm=tm),
        grid=(b, t // tm),
        in_specs=[pl.BlockSpec((None, tm, D_MODEL), lambda bi, ti: (bi, ti, 0)),
                  pl.BlockSpec(w1.shape, lambda bi, ti: (0, 0), pipeline_mode=pl.Buffered(1))],
        out_specs=out_specs,
        out_shape=out_shape,
        scratch_shapes=[pltpu.VMEM((D_MODEL // LANES, tm, LANES), F32)],
        compiler_params=_cparams(("parallel", "parallel"), 48 << 20),
        name="l1_proj_streams",
    )(x3, w1)


def _proj_t_kernel(x_ref, w_ref, o_ref):
    o_ref[...] = _dot_t(w_ref[...], x_ref[...].astype(BF16))


def _proj_t(x3, w_t, window, tm):
    b, t, k = x3.shape
    f = w_t.shape[0]
    first = (t - window) // tm
    return pl.pallas_call(
        _proj_t_kernel,
        grid=(b, window // tm),
        in_specs=[pl.BlockSpec((None, tm, k), lambda bi, ti: (bi, first + ti, 0)),
                  pl.BlockSpec(w_t.shape, lambda bi, ti: (0, 0))],
        out_specs=pl.BlockSpec((None, f, tm), lambda bi, ti: (bi, 0, ti)),
        out_shape=jax.ShapeDtypeStruct((b, f, window), F32),
        compiler_params=_cparams(("parallel", "parallel"), 32 << 20),
        name="proj_t",
    )(x3, w_t)


def _alibi_slope(group, head):
    n = C_GROUPS * N_HEADS
    return float(2.0 ** (-8.0 * (group * N_HEADS + head + 1) / n))


def _dil_attn_kernel(q_ref, kv_ref, o_ref, lse_ref, *, group, n_blocks):
    r = C_DILATIONS[group]
    blk = C_STEPS

    def band(kn):
        qq = _iota((2 * blk, kn), 0) & (blk - 1)
        steps = qq - _iota((2 * blk, kn), 1) + (kn - blk)
        return (steps >= 0) & (steps <= C_STEPS), (r * steps).astype(F32)

    def block(si, n, first):
        qs = pl.multiple_of(n * blk, blk)
        ks = 0 if first else pl.multiple_of((n - 1) * blk, blk)
        kn = blk if first else 2 * blk
        valid, stepsf = band(kn)
        odd_row = _iota((2 * blk, 1), 0) >= blk
        even = _pair_half_mask(blk, 0)
        for p in range(N_PAIRS):
            qpair = q_ref[si, pl.ds(qs, blk), p * PAIR:(p + 1) * PAIR]
            kb = kv_ref[si, pl.ds(ks, kn), p * PAIR:(p + 1) * PAIR]
            vb = kv_ref[si, pl.ds(ks, kn), WIDTH + p * PAIR:WIDTH + (p + 1) * PAIR]
            q2 = jnp.concatenate([_keep(_pair_half_mask(blk, 0), qpair), _keep(_pair_half_mask(blk, 1), qpair)], axis=0)
            slope = jnp.where(odd_row, _alibi_slope(group, 2 * p + 1), _alibi_slope(group, 2 * p))
            s = jnp.where(valid, _dot_t(q2, kb) - slope * stepsf, NEG_INF)
            mx = jnp.max(s, axis=1, keepdims=True)
            pr = jnp.exp(s - mx)
            den = jnp.sum(pr, axis=1, keepdims=True)
            o2 = _dot(pr.astype(BF16), vb) / den
            lse2 = jnp.broadcast_to(mx + jnp.log(den), (2 * blk, PAIR))
            o_ref[si, pl.ds(qs, blk), p * PAIR:(p + 1) * PAIR] = jnp.where(even, o2[:blk], o2[blk:])
            lse_ref[si, pl.ds(qs, blk), p * PAIR:(p + 1) * PAIR] = jnp.where(even, lse2[:blk], lse2[blk:])

    def stream(si, c):
        block(si, 0, True)
        if n_blocks > 1:
            def body(n, cc):
                block(si, n, False)
                return cc
            lax.fori_loop(1, n_blocks, body, 0)
        return c

    if r == 1:
        stream(0, 0)
    else:
        lax.fori_loop(0, r, stream, 0)


def _dil_attn(q, kvb, group):
    b, r, ls, _ = q.shape
    spec = lambda n: pl.BlockSpec((None, r, ls, n), lambda bi: (bi, 0, 0, 0))
    return pl.pallas_call(
        functools.partial(_dil_attn_kernel, group=group, n_blocks=ls // C_STEPS),
        grid=(b,),
        in_specs=[spec(WIDTH), spec(2 * WIDTH)],
        out_specs=[spec(WIDTH)] * 2,
        out_shape=[jax.ShapeDtypeStruct((b, r, ls, WIDTH), F32)] * 2,
        compiler_params=_cparams(("parallel",), 48 << 20),
        name=f"dil_attn{group}",
    )(q, kvb)


def _block_diag_rows(row, width):
    sel = _head_of_lane((DEC_ROWS, width), 1) == _iota((DEC_ROWS, width), 0)
    return jnp.where(sel, jnp.broadcast_to(row.astype(F32), (DEC_ROWS, width)), 0.0).astype(row.dtype)


def _diag_heads(full):
    sel = _head_of_lane(full.shape, 1) == _iota(full.shape, 0)
    return jnp.sum(jnp.where(sel, full, 0.0), axis=0, keepdims=True)


def _sb_decode_kernel(pt_ref, q_ref, u_ref, cache_ref, o_ref, buf_ref, sem_ref, acc_ref, carry_ref, *, n_pages):
    b = pl.program_id(0)

    def page_copy(p, slot, seq=b):
        return pltpu.make_async_copy(cache_ref.at[pt_ref[seq, p]], buf_ref.at[slot], sem_ref.at[slot])

    def start_latest(seq):
        page_copy(n_pages - 1, (n_pages - 1) % 2, seq).start()
        if n_pages > 1:
            page_copy(n_pages - 2, (n_pages - 2) % 2, seq).start()

    @pl.when(b == 0)
    def _():
        start_latest(b)

    acc_ref[...] = jnp.zeros_like(acc_ref)
    carry_ref[...] = jnp.zeros_like(carry_ref)
    qbd = _block_diag_rows(q_ref[...], WIDTH)
    u = u_ref[...]

    def cond(c):
        p, mx = c
        return (p >= 0) & (mx >= SB_EXIT)

    def body(c):
        p, _ = c
        slot = lax.rem(p, 2)
        page_copy(p, slot).wait()
        kt = buf_ref[slot, 0].astype(BF16)
        vt = buf_ref[slot, 1].astype(BF16)
        z = _dot(qbd, kt)
        sp = _softplus(z)
        lk = -sp
        hi, lo = _split_bf16(lk)
        tail = _dot(hi, u) + _dot(lo, u)
        w = jnp.exp(z - sp + tail + carry_ref[...])
        acc_ref[...] += _dot_t(w.astype(BF16), vt)
        carry_ref[...] += jnp.sum(lk, axis=1, keepdims=True)
        mx = jnp.max(carry_ref[:N_HEADS, :])

        @pl.when((mx >= SB_EXIT) & (p >= 2))
        def _():
            page_copy(p - 2, slot).start()

        return p - 1, mx

    p_end, _ = lax.while_loop(cond, body, (jnp.int32(n_pages - 1), jnp.float32(0.0)))

    @pl.when(p_end >= 0)
    def _():
        page_copy(p_end, lax.rem(p_end, 2)).wait()

    @pl.when(b + 1 < pl.num_programs(0))
    def _():
        start_latest(b + 1)

    o_ref[...] = _diag_heads(acc_ref[...])


def _sb_decode(page_table, cache_t, qa_s, umat):
    db, n_pages = page_table.shape
    page_shape = cache_t.shape[1:]
    grid_spec = pltpu.PrefetchScalarGridSpec(
        num_scalar_prefetch=1,
        grid=(db,),
        in_specs=[pl.BlockSpec((None, 1, WIDTH), lambda b, pt: (b, 0, 0)),
                  pl.BlockSpec(umat.shape, lambda b, pt: (0, 0)),
                  pl.BlockSpec(memory_space=pl.ANY)],
        out_specs=pl.BlockSpec((None, 1, WIDTH), lambda b, pt: (b, 0, 0)),
        scratch_shapes=[pltpu.VMEM((2,) + page_shape, F32), pltpu.SemaphoreType.DMA((2,)),
                        pltpu.VMEM((DEC_ROWS, WIDTH), F32), pltpu.VMEM((DEC_ROWS, 1), F32)],
    )
    return pl.pallas_call(
        functools.partial(_sb_decode_kernel, n_pages=n_pages),
        grid_spec=grid_spec,
        out_shape=jax.ShapeDtypeStruct((db, 1, WIDTH), F32),
        compiler_params=_cparams(("arbitrary",), 16 << 20),
        name="sb_decode",
    )(page_table, qa_s.reshape(db, 1, WIDTH), umat, cache_t)


def _mla_decode_kernel(pt_ref, qn_ref, qr_ref, lat_ref, wuk_ref, wuv_ref, gsel_ref, *refs, n_pages):
    page_refs = refs[:n_pages]
    o_ref, ql_ref, qrope_ref, acc_ref, m_ref, l_ref = refs[n_pages:]
    s = pl.program_id(1)

    @pl.when(s == 0)
    def _():
        ql = _dot(_block_diag_rows(qn_ref[...], WIDTH), wuk_ref[...]).astype(BF16)
        shape = (DEC_ROWS, 2 * LANES)
        lane, rowi = _iota(shape, 1), _iota(shape, 0)
        sel = (((lane >= LANES) == (rowi >= 4)) & ((lax.shift_right_logical(lane, 4) & 3) == (rowi & 3))
               & (rowi < N_HEADS))
        qsel = jnp.where(sel, jnp.broadcast_to(qr_ref[...].astype(F32), shape), 0.0).astype(BF16)
        qrope = _dot(qsel, gsel_ref[...]).astype(BF16)
        ql_ref[...] = ql
        qrope_ref[...] = qrope
        own = lat_ref[...].astype(BF16).astype(F32)
        s_own = (jnp.sum(ql.astype(F32) * own[:, :B_KV_RANK], axis=1, keepdims=True)
                 + jnp.sum(qrope.astype(F32) * own[:, B_KV_RANK:], axis=1, keepdims=True)) * MLA_SCALE
        m_ref[...] = s_own
        l_ref[...] = jnp.ones_like(l_ref)
        acc_ref[...] = jnp.broadcast_to(own[:, :B_KV_RANK], acc_ref.shape)

    ckv = jnp.concatenate([r[:B_KV_RANK, :].astype(BF16) for r in page_refs], axis=1)
    kro = jnp.concatenate([r[B_KV_RANK:, :].astype(BF16) for r in page_refs], axis=1)
    sc = (_dot(ql_ref[...], ckv) + _dot(qrope_ref[...], kro)) * MLA_SCALE
    m_old = m_ref[...]
    m_new = jnp.maximum(m_old, jnp.max(sc, axis=1, keepdims=True))
    pr = jnp.exp(sc - m_new)
    alpha = jnp.exp(m_old - m_new)
    l_ref[...] = alpha * l_ref[...] + jnp.sum(pr, axis=1, keepdims=True)
    acc_ref[...] = alpha * acc_ref[...] + _dot_t(pr.astype(BF16), ckv)
    m_ref[...] = m_new

    @pl.when(s == pl.num_programs(1) - 1)
    def _():
        o_lat = (acc_ref[...] / l_ref[...]).astype(BF16)
        o_ref[...] = _diag_heads(_dot(o_lat, wuv_ref[...]))


def _mla_decode(page_table, cache_t, qn_s, qr_s, lat_s, wuk_t, wuv, gsel, n_pages):
    db, pages_per_seq = page_table.shape
    n_steps = pages_per_seq // n_pages
    page_rows = cache_t.shape[2]

    def page_spec(k):
        return pl.BlockSpec((None, B_LATENT, page_rows), lambda b, s, pt: (pt[b, s * n_pages + k], 0, 0))

    per_b = lambda n: pl.BlockSpec((None, 1, n), lambda b, s, pt: (b, 0, 0))
    full = lambda a: pl.BlockSpec(a.shape, lambda b, s, pt: (0,) * a.ndim)
    grid_spec = pltpu.PrefetchScalarGridSpec(
        num_scalar_prefetch=1,
        grid=(db, n_steps),
        in_specs=[per_b(WIDTH), per_b(2 * LANES), per_b(B_LATENT), full(wuk_t), full(wuv), full(gsel)]
                 + [page_spec(k) for k in range(n_pages)],
        out_specs=per_b(WIDTH),
        scratch_shapes=[pltpu.VMEM((DEC_ROWS, B_KV_RANK), BF16), pltpu.VMEM((DEC_ROWS, B_ROPE), BF16),
                        pltpu.VMEM((DEC_ROWS, B_KV_RANK), F32), pltpu.VMEM((DEC_ROWS, 1), F32),
                        pltpu.VMEM((DEC_ROWS, 1), F32)],
    )
    return pl.pallas_call(
        functools.partial(_mla_decode_kernel, n_pages=n_pages),
        grid_spec=grid_spec,
        out_shape=jax.ShapeDtypeStruct((db, 1, WIDTH), F32),
        compiler_params=_cparams(("parallel", "arbitrary"), 40 << 20),
        name="mla_decode",
    )(page_table, qn_s.reshape(db, 1, WIDTH), qr_s.reshape(db, 1, 2 * LANES), lat_s.reshape(db, 1, B_LATENT),
      wuk_t, wuv, gsel, *([cache_t] * n_pages))


def _dil_decode_kernel(q_ref, kvn_ref, c0_ref, c1_ref, c2_ref, o_ref):
    caches = (c0_ref, c1_ref, c2_ref)
    head = _iota((DEC_ROWS, 1), 0).astype(F32)
    outs, lses = [], []
    for g in range(C_GROUPS):
        r = C_DILATIONS[g]
        win = caches[g].shape[-1]
        slope = jnp.exp((-8.0 * (g * N_HEADS + head + 1.0) / (C_GROUPS * N_HEADS)) * np.log(2.0))
        qbd = _block_diag_rows(q_ref[:, g * WIDTH:(g + 1) * WIDTH], WIDTH)
        kt = caches[g][0].astype(BF16)
        vt = caches[g][1].astype(BF16)
        k_new = kvn_ref[:, g * 2 * WIDTH:g * 2 * WIDTH + WIDTH].astype(BF16)
        v_new = kvn_ref[:, g * 2 * WIDTH + WIDTH:(g + 1) * 2 * WIDTH].astype(BF16).astype(F32)
        row = _iota((DEC_ROWS, win), 1)
        s_win = _dot(qbd, kt) - slope * (win - row).astype(F32)
        s_win = jnp.where((row & (r - 1)) == 0, s_win, NEG_INF)
        s_own = jnp.sum(qbd.astype(F32) * k_new.astype(F32), axis=1, keepdims=True)
        mx = jnp.maximum(jnp.max(s_win, axis=1, keepdims=True), s_own)
        p_win = jnp.exp(s_win - mx)
        p_own = jnp.exp(s_own - mx)
        den = jnp.sum(p_win, axis=1, keepdims=True) + p_own
        o = (_dot_t(p_win.astype(BF16), vt) + p_own.astype(BF16).astype(F32) * v_new) / den
        outs.append(o)
        lses.append(mx + jnp.log(den))
    mx = functools.reduce(jnp.maximum, lses)
    es = [jnp.exp(l - mx) for l in lses]
    tot = functools.reduce(lambda a, c: a + c, es)
    merged = functools.reduce(lambda a, c: a + c, [(e / tot) * o for e, o in zip(es, outs)])
    o_ref[...] = _diag_heads(merged)


def _dil_decode(q3, kvn3, caches_t):
    db = q3.shape[0]
    per_b = lambda n: pl.BlockSpec((None, 1, n), lambda b: (b, 0, 0))
    specs = []
    for g, c in enumerate(caches_t):
        assert c.shape[-1] == C_STEPS * C_DILATIONS[g]
        specs.append(pl.BlockSpec((None,) + c.shape[1:], lambda b: (b, 0, 0, 0)))
    return pl.pallas_call(
        _dil_decode_kernel,
        grid=(db,),
        in_specs=[per_b(C_GROUPS * WIDTH), per_b(C_GROUPS * 2 * WIDTH)] + specs,
        out_specs=per_b(WIDTH),
        out_shape=jax.ShapeDtypeStruct((db, 1, WIDTH), F32),
        compiler_params=_cparams(("parallel",), 48 << 20),
        name="dil_decode",
    )(q3.reshape(db, 1, C_GROUPS * WIDTH), kvn3.reshape(db, 1, C_GROUPS * 2 * WIDTH), *caches_t)


def _prep_w_in0(w_in0):
    cuts = np.cumsum(IN0_SIZES)[:-1].tolist()
    qa, ka, va, dq, dkv, kr, gate = jnp.split(w_in0, cuts, axis=1)
    kr_sw = jnp.concatenate([kr[:, B_HALF:], kr[:, :B_HALF]], axis=1)
    wq = jnp.concatenate([qa, gate, dq], axis=1).astype(BF16)
    wk = jnp.concatenate([ka, va, dkv, kr, kr_sw], axis=1).T.astype(BF16)
    return wq, wk


def _prep_w_uq(w_uq):
    w = w_uq.reshape(B_Q_RANK, N_HEADS, B_NOPE + B_ROPE)
    nope = w[:, :, :B_NOPE].reshape(B_Q_RANK, WIDTH)
    x1 = w[:, :, B_NOPE:B_NOPE + B_HALF]
    x2 = w[:, :, B_NOPE + B_HALF:]
    slabs, slabs_sw = [], []
    for s in range(2):
        a = x1[:, 4 * s:4 * s + 4].reshape(B_Q_RANK, 4 * B_HALF)
        c = x2[:, 4 * s:4 * s + 4].reshape(B_Q_RANK, 4 * B_HALF)
        slabs.append(jnp.concatenate([a, c], axis=1))
        slabs_sw.append(jnp.concatenate([c, a], axis=1))
    return jnp.concatenate([nope] + slabs + slabs_sw, axis=1).astype(BF16)


def _rope_tables(pos):
    inv = ROPE_THETA ** (-jnp.arange(0, B_ROPE, 2, dtype=F32) / B_ROPE)
    ang = pos.astype(F32)[:, None] * inv[None, :]
    cos, sin = jnp.cos(ang), jnp.sin(ang)
    tabq = jnp.concatenate([jnp.tile(cos, (1, 8)), jnp.tile(-sin, (1, 4)), jnp.tile(sin, (1, 4))], axis=1)
    tabk = jnp.concatenate([cos, cos, -sin, sin], axis=1).T
    return tabq, tabk


def _const_mats():
    tri = lambda n: jnp.asarray((np.arange(n)[:, None] > np.arange(n)[None, :]).astype(np.float32), BF16)
    emat = np.zeros((B_ROPE, LANES), np.float32)
    for c in range(LANES):
        emat[(c % B_HALF) + (B_HALF if c >= 4 * B_HALF else 0), c] = 1.0
    gsel = np.zeros((2 * LANES, B_ROPE), np.float32)
    for c in range(2 * LANES):
        cc = c % LANES
        gsel[c, (cc % B_HALF) + (B_HALF if cc >= 4 * B_HALF else 0)] = 1.0
    return tri(MXU_TILE), tri(LANES), jnp.asarray(emat.T, BF16), jnp.asarray(gsel, BF16)


def _kv_rows(a_t, n, t):
    return jnp.transpose(a_t.reshape(n, 2, N_HEADS, HEAD_DIM, t), (0, 4, 1, 2, 3))


def _kv_pages(c):
    return jnp.transpose(c, (0, 2, 3, 4, 1)).reshape(c.shape[0], 2, WIDTH, c.shape[1])


def kernel(x_prompt, x_sample, cache_a_kv, cache_b_latent, cache_c0_kv, cache_c1_kv, cache_c2_kv,
           page_table, w_in0, g_cq, g_ckv, w_uq, w_uk, w_uv, w_out0, ln0_g, ln0_b,
           w_in1, w_out1, ln1_g, ln1_b):
    bsz, t_p, _ = x_prompt.shape
    db, t_s, _ = x_sample.shape
    assert t_s == 1 and t_p % MXU_TILE == 0 and cache_a_kv.shape[1] == LANES
    past_len = page_table.shape[1] * cache_a_kv.shape[1]
    m_p, m_s = bsz * t_p, db * t_s
    tm_p, tm_s = MXU_TILE, m_s

    tri_blk, tri_page, emat_t, gsel = _const_mats()
    wq0, wk0 = _prep_w_in0(w_in0)
    wuq = _prep_w_uq(w_uq)
    wuk_b, wuv_b = w_uk.astype(BF16), w_uv.astype(BF16)
    wo0, wo1, w1 = w_out0.astype(BF16), w_out1.astype(BF16), w_in1.astype(BF16)
    row = lambda v: v.reshape(1, -1)
    tabq_p, tabk_p = _rope_tables(jnp.arange(t_p))
    tabq_s, tabk_s = _rope_tables(jnp.full((m_s,), past_len, jnp.int32))

    xp = x_prompt.reshape(m_p, D_MODEL)
    xs3 = x_sample.reshape(1, m_s, D_MODEL)
    xs = xs3.reshape(m_s, D_MODEL)

    proj = lambda x3, tq_, tk_, tm: _l0_proj(x3, tq_, tk_, wq0, wk0, row(g_cq), g_ckv.reshape(-1, 1), wuq,
                                             wuk_b.T, wuv_b.T, emat_t, tm)
    qa_p, gate_p, qn_p, qr_p, kvf_p, lat_p, kvb_p, kcat_p, vb_p = proj(x_prompt, tabq_p, tabk_p, tm_p)
    qa_s, gate_s, qn_s, qr_s, kvf_s, lat_s, _, _, _ = proj(xs3, tabq_s, tabk_s, tm_s)

    oa_p = _sb_attn(qa_p, kvb_p, tri_blk).reshape(m_p, WIDTH)
    ob_p = _mla_attn(qn_p, qr_p, kcat_p, vb_p).reshape(m_p, WIDTH)
    h_p = _finish0(xp, oa_p, ob_p, gate_p.reshape(m_p, 2 * WIDTH), wo0, row(ln0_g), row(ln0_b), tm_p)

    lat_s_rows = lat_s[0].T
    oa_s = _sb_decode(page_table, _kv_pages(cache_a_kv), qa_s[0], tri_page).reshape(m_s, WIDTH)
    ob_s = _mla_decode(page_table, jnp.transpose(cache_b_latent, (0, 2, 1)), qn_s[0], qr_s[0], lat_s_rows,
                       wuk_b.T, wuv_b, gsel, n_pages=min(32, page_table.shape[1])).reshape(m_s, WIDTH)
    h_s = _finish0(xs, oa_s, ob_s, gate_s[0], wo0, row(ln0_g), row(ln0_b), tm_s)

    h_p3 = h_p.reshape(bsz, t_p, D_MODEL)
    outs_p = _l1_proj_streams(h_p3, w1, tm_p)
    q_p, kvb1_p, gate1_p = outs_p[0:3], outs_p[3:6], outs_p[6]
    os_p, lses_p, c_new_p = [], [], []
    for g, c in enumerate((cache_c0_kv, cache_c1_kv, cache_c2_kv)):
        o, lse = _dil_attn(q_p[g], kvb1_p[g], g)
        os_p.append(o)
        lses_p.append(lse)
        win = min(c.shape[1], t_p)
        w_kv_t = w1[:, g * 3 * WIDTH + WIDTH:(g + 1) * 3 * WIDTH].T
        c_new_p.append(_kv_rows(_proj_t(h_p3, w_kv_t, win, min(win, MXU_TILE)), bsz, win))
    y_p = _finish1_streams(h_p3, os_p, lses_p, gate1_p, wo1, row(ln1_g), row(ln1_b), tm_p)

    outs_s = _l1_proj(h_s, w1, tm_s, F32)
    q_s, kvf1_s, gate1_s = outs_s[0:3], outs_s[3:6], outs_s[6]
    oc_s = _dil_decode(jnp.concatenate(q_s, axis=1), jnp.concatenate(kvf1_s, axis=1),
                       [_kv_pages(c) for c in (cache_c0_kv, cache_c1_kv, cache_c2_kv)]).reshape(m_s, WIDTH)
    y_s = _finish1(h_s, [oc_s], [], gate1_s, wo1, row(ln1_g), row(ln1_b), tm_s)

    kv5 = lambda a: a.reshape(db, t_s, 2, N_HEADS, HEAD_DIM)
    return (y_p.reshape(bsz, t_p, D_MODEL), y_s.reshape(db, t_s, D_MODEL),
            _kv_rows(kvf_p, bsz, t_p), kv5(kvf_s[0].T),
            jnp.transpose(lat_p, (0, 2, 1)), lat_s_rows.reshape(db, t_s, B_LATENT),
            c_new_p[0], kv5(kvf1_s[0]), c_new_p[1], kv5(kvf1_s[1]), c_new_p[2], kv5(kvf1_s[2]))
```

```python
import functools

import numpy as np
import jax
import jax.numpy as jnp
from jax import lax
from jax.experimental import pallas as pl
from jax.experimental.pallas import tpu as pltpu

D_MODEL = 1024
DEPTH = 2
HEAD_DIM = 64
N_HEADS = 8
WIDTH = N_HEADS * HEAD_DIM
ATTN_SCALE = HEAD_DIM ** -0.5
B_NOPE = 64
B_ROPE = 32
B_HALF = B_ROPE // 2
B_Q_RANK = 384
B_KV_RANK = 256
B_LATENT = B_KV_RANK + B_ROPE
MLA_SCALE = (B_NOPE + B_ROPE) ** -0.5
ROPE_THETA = 10000.0
C_DILATIONS = (1, 4, 16)
C_GROUPS = 3
C_STEPS = 128
NORM_EPS = 1e-5
DN_ALPHA = (2 * DEPTH) ** 0.25
IN0_SIZES = (WIDTH, WIDTH, WIDTH, B_Q_RANK, B_KV_RANK, B_ROPE, 2 * WIDTH)

LANES = 128
PAIR = 2 * HEAD_DIM
N_PAIRS = N_HEADS // 2
V7X_VMEM_BYTES = 64 * 1024 * 1024
MXU_TILE = 256
DEC_ROWS = 16

NEG_INF = float("-inf")
SB_EXIT = -104.0
F32 = jnp.float32
BF16 = jnp.bfloat16


def _cparams(semantics, vmem_bytes):
    return pltpu.CompilerParams(dimension_semantics=semantics,
                                vmem_limit_bytes=min(int(vmem_bytes), V7X_VMEM_BYTES - (4 << 20)))


def _dot(a, b):
    return jnp.dot(a, b, preferred_element_type=F32)


def _dot_t(a, b):
    return lax.dot_general(a, b, (((1,), (1,)), ((), ())), preferred_element_type=F32)


def _iota(shape, dim):
    return lax.broadcasted_iota(jnp.int32, shape, dim)


def _softplus(z):
    return jnp.maximum(z, 0.0) + jnp.log(1.0 + jnp.exp(-jnp.abs(z)))


def _split_bf16(x):
    hi = x.astype(BF16)
    lo = (x - hi.astype(F32)).astype(BF16)
    return hi, lo


def _pair_half_mask(rows, head):
    lane = _iota((rows, PAIR), 1)
    return (lane >= HEAD_DIM) if head % 2 else (lane < HEAD_DIM)


def _keep(mask, x):
    return jnp.where(mask, x.astype(F32), 0.0).astype(x.dtype)


def _head_of_lane(shape, dim):
    return lax.shift_right_logical(_iota(shape, dim), 6)


def _twice(x):
    return jnp.concatenate([x, x], axis=1)


_Q_QA, _Q_GATE, _Q_DQ, _N_Q0 = 0, WIDTH, 3 * WIDTH, 3 * WIDTH + B_Q_RANK
_K_KV, _K_DKV, _K_KR, _N_K0 = 0, 2 * WIDTH, 2 * WIDTH + B_KV_RANK, 2 * WIDTH + B_KV_RANK + 2 * B_ROPE
_N_UQ = WIDTH + 4 * LANES


def _l0_proj_kernel(x_ref, tabq_ref, tabk_ref, wq_ref, wk_ref, gcq_ref, gckv_ref, wuq_ref, wuk_ref, wuv_ref,
                    e_ref, qa_ref, gate_ref, qn_ref, qr_ref, kvf_ref, lat_ref, kvb_ref, kcat_ref, vb_ref):
    xb = x_ref[...].astype(BF16)
    qa_ref[...] = (_dot(xb, wq_ref[:, _Q_QA:_Q_GATE]) * ATTN_SCALE).astype(BF16)
    gate_ref[...] = _dot(xb, wq_ref[:, _Q_GATE:_Q_DQ])
    dq = _dot(xb, wq_ref[:, _Q_DQ:_N_Q0])
    dqn = dq * lax.rsqrt(jnp.mean(dq * dq, axis=-1, keepdims=True) + NORM_EPS) * gcq_ref[...]
    qf = _dot(dqn.astype(BF16), wuq_ref[...])
    qn_ref[...] = qf[:, :WIDTH].astype(BF16)
    cosq = tabq_ref[:, :LANES]
    sinq = tabq_ref[:, LANES:]
    for s in range(2):
        nat = qf[:, WIDTH + s * LANES:WIDTH + (s + 1) * LANES]
        swp = qf[:, WIDTH + (2 + s) * LANES:WIDTH + (3 + s) * LANES]
        qr_ref[:, s * LANES:(s + 1) * LANES] = (nat * cosq + swp * sinq).astype(BF16)
    kv = _dot_t(wk_ref[_K_KV:_K_DKV, :], xb)
    kvf_ref[...] = kv
    kvb_ref[...] = kv.astype(BF16)
    dkv = _dot_t(wk_ref[_K_DKV:_K_KR, :], xb)
    ckv = dkv * lax.rsqrt(jnp.mean(dkv * dkv, axis=0, keepdims=True) + NORM_EPS) * gckv_ref[...]
    lat_ref[:B_KV_RANK, :] = ckv
    kr2 = _dot_t(wk_ref[_K_KR:_N_K0, :], xb)
    krot = kr2[:B_ROPE, :] * tabk_ref[:B_ROPE, :] + kr2[B_ROPE:, :] * tabk_ref[B_ROPE:, :]
    lat_ref[B_KV_RANK:, :] = krot
    ckv_b = ckv.astype(BF16)
    kn = _dot(wuk_ref[...], ckv_b).astype(BF16)
    vb_ref[...] = _dot(wuv_ref[...], ckv_b).astype(BF16)
    kslab = _dot(e_ref[...], krot.astype(BF16)).astype(BF16)
    for p in range(N_PAIRS):
        kcat_ref[p * MXU_TILE:p * MXU_TILE + PAIR, :] = kn[p * PAIR:(p + 1) * PAIR, :]
        kcat_ref[p * MXU_TILE + PAIR:(p + 1) * MXU_TILE, :] = kslab


def _l0_proj(x3, tabq, tabk, wq, wk, gcq, gckv, wuq, wuk_t, wuv_t, emat_t, tm):
    b, t, _ = x3.shape
    nt = t // tm
    tok = lambda n: pl.BlockSpec((None, tm, n), lambda bi, ti: (bi, ti, 0))
    feat = lambda n: pl.BlockSpec((None, n, tm), lambda bi, ti: (bi, 0, ti))
    featb = lambda n: pl.BlockSpec((None, None, n, tm), lambda bi, ti: (bi, ti, 0, 0))
    full = lambda a: pl.BlockSpec(a.shape, lambda bi, ti: (0,) * a.ndim, pipeline_mode=pl.Buffered(1))
    s = jax.ShapeDtypeStruct
    return pl.pallas_call(
        _l0_proj_kernel,
        grid=(b, nt),
        in_specs=[tok(D_MODEL), pl.BlockSpec((tm, 2 * LANES), lambda bi, ti: (ti, 0)),
                  pl.BlockSpec((2 * B_ROPE, tm), lambda bi, ti: (0, ti)),
                  full(wq), full(wk), full(gcq), full(gckv), full(wuq), full(wuk_t), full(wuv_t), full(emat_t)],
        out_specs=[tok(WIDTH), tok(2 * WIDTH), tok(WIDTH), tok(2 * LANES),
                   feat(2 * WIDTH), feat(B_LATENT),
                   featb(2 * WIDTH), featb(N_PAIRS * MXU_TILE), featb(WIDTH)],
        out_shape=[s((b, t, WIDTH), BF16), s((b, t, 2 * WIDTH), F32), s((b, t, WIDTH), BF16),
                   s((b, t, 2 * LANES), BF16),
                   s((b, 2 * WIDTH, t), F32), s((b, B_LATENT, t), F32),
                   s((b, nt, 2 * WIDTH, tm), BF16), s((b, nt, N_PAIRS * MXU_TILE, tm), BF16),
                   s((b, nt, WIDTH, tm), BF16)],
        compiler_params=_cparams(("parallel", "parallel"), 48 << 20),
        name="l0_proj",
    )(x3, tabq, tabk, wq, wk, gcq, gckv, wuq, wuk_t, wuv_t, emat_t)


def _sb_attn_kernel(q_ref, kv_ref, u_ref, o_ref, qp_ref, acc_ref, carry_ref, *, tq):
    i = pl.program_id(1)
    acc_ref[...] = jnp.zeros_like(acc_ref)
    carry_ref[...] = jnp.zeros_like(carry_ref)
    for h in range(N_HEADS):
        qp_ref[h // 2, (h % 2) * tq:(h % 2 + 1) * tq, :] = _keep(
            _pair_half_mask(tq, h), q_ref[:, (h // 2) * PAIR:(h // 2 + 1) * PAIR])

    def block(j, masked):
        u = u_ref[...]
        if masked:
            strict = _iota((2 * tq, tq), 1) < (_iota((2 * tq, tq), 0) & (tq - 1))
        for p in range(N_PAIRS):
            kt = kv_ref[j, p * PAIR:(p + 1) * PAIR, :]
            vt = kv_ref[j, WIDTH + p * PAIR:WIDTH + (p + 1) * PAIR, :]
            z = _dot(qp_ref[p], kt)
            sp = _softplus(z)
            lk = jnp.where(strict, -sp, 0.0) if masked else -sp
            hi, lo = _split_bf16(lk)
            tail = _dot(hi, u) + _dot(lo, u)
            w = jnp.exp(z - sp + tail + _twice(carry_ref[p]))
            if masked:
                w = jnp.where(strict, w, 0.0)
            acc_ref[p] += _dot_t(w.astype(BF16), vt)
            carry_ref[p] += jnp.sum(lk, axis=1, keepdims=True)

    def least_decayed():
        m = carry_ref[0]
        for p in range(1, N_PAIRS):
            m = jnp.maximum(m, carry_ref[p])
        return jnp.max(m)

    block(i, True)

    def cond(c):
        jj, mx = c
        return (jj <= i) & (mx >= SB_EXIT)

    def body(c):
        jj, _ = c
        block(i - jj, False)
        return jj + 1, least_decayed()

    lax.while_loop(cond, body, (jnp.int32(1), least_decayed()))
    for p in range(N_PAIRS):
        o_ref[:, p * PAIR:(p + 1) * PAIR] = jnp.where(
            _pair_half_mask(tq, 0), acc_ref[p, :tq, :], acc_ref[p, tq:, :]).astype(o_ref.dtype)


def _sb_attn(qa, kvb, umat):
    b, t, _ = qa.shape
    _, nblk, _, tq = kvb.shape
    return pl.pallas_call(
        functools.partial(_sb_attn_kernel, tq=tq),
        grid=(b, nblk),
        in_specs=[pl.BlockSpec((None, tq, WIDTH), lambda bi, i: (bi, i, 0)),
                  pl.BlockSpec((None, nblk, 2 * WIDTH, tq), lambda bi, i: (bi, 0, 0, 0)),
                  pl.BlockSpec(umat.shape, lambda bi, i: (0, 0))],
        out_specs=pl.BlockSpec((None, tq, WIDTH), lambda bi, i: (bi, i, 0)),
        out_shape=jax.ShapeDtypeStruct((b, t, WIDTH), BF16),
        scratch_shapes=[pltpu.VMEM((N_PAIRS, 2 * tq, PAIR), BF16), pltpu.VMEM((N_PAIRS, 2 * tq, PAIR), F32),
                        pltpu.VMEM((N_PAIRS, 2 * tq, PAIR), F32)],
        compiler_params=_cparams(("parallel", "arbitrary"), 40 << 20),
        name="sb_attn",
    )(qa, kvb, umat)


def _rope_slab_mask(rows, head):
    lane = _iota((rows, LANES), 1)
    return (lax.shift_right_logical(lane, 4) & 3) == (head % 4)


def _mla_attn_kernel(qn_ref, qr_ref, kcat_ref, vb_ref, o_ref, qc_ref, acc_ref, m_ref, l_ref, *, tq):
    i = pl.program_id(1)
    acc_ref[...] = jnp.zeros_like(acc_ref)
    m_ref[...] = jnp.full_like(m_ref, NEG_INF)
    l_ref[...] = jnp.zeros_like(l_ref)
    for h in range(N_HEADS):
        rows = slice((h % 2) * tq, (h % 2 + 1) * tq)
        qc_ref[h // 2, rows, :PAIR] = _keep(_pair_half_mask(tq, h), qn_ref[:, (h // 2) * PAIR:(h // 2 + 1) * PAIR])
        qc_ref[h // 2, rows, PAIR:] = _keep(_rope_slab_mask(tq, h), qr_ref[:, (h // 4) * LANES:(h // 4 + 1) * LANES])

    def block(j, masked):
        if masked:
            causal = _iota((2 * tq, tq), 1) <= (_iota((2 * tq, tq), 0) & (tq - 1))
        for p in range(N_PAIRS):
            kt = kcat_ref[j, p * MXU_TILE:(p + 1) * MXU_TILE, :]
            vt = vb_ref[j, p * PAIR:(p + 1) * PAIR, :]
            s = _dot(qc_ref[p], kt) * MLA_SCALE
            if masked:
                s = jnp.where(causal, s, NEG_INF)
            m_old = m_ref[p]
            m_new = jnp.maximum(m_old, jnp.max(s, axis=1, keepdims=True))
            pr = jnp.exp(s - _twice(m_new))
            alpha = jnp.exp(m_old - m_new)
            l_ref[p] = alpha * l_ref[p] + jnp.sum(pr, axis=1, keepdims=True)
            acc_ref[p] = alpha * acc_ref[p] + _dot_t(pr.astype(BF16), vt)
            m_ref[p] = m_new

    block(i, True)

    def body(j, c):
        block(j, False)
        return c

    lax.fori_loop(0, i, body, 0)
    for p in range(N_PAIRS):
        o_ref[:, p * PAIR:(p + 1) * PAIR] = jnp.where(
            _pair_half_mask(tq, 0), acc_ref[p, :tq, :] / l_ref[p, :tq, :],
            acc_ref[p, tq:, :] / l_ref[p, tq:, :]).astype(o_ref.dtype)


def _mla_attn(qn, qr, kcat, vb):
    b, t, _ = qn.shape
    _, nblk, _, tq = kcat.shape
    return pl.pallas_call(
        functools.partial(_mla_attn_kernel, tq=tq),
        grid=(b, nblk),
        in_specs=[pl.BlockSpec((None, tq, WIDTH), lambda bi, i: (bi, i, 0)),
                  pl.BlockSpec((None, tq, 2 * LANES), lambda bi, i: (bi, i, 0)),
                  pl.BlockSpec((None, nblk, N_PAIRS * MXU_TILE, tq), lambda bi, i: (bi, 0, 0, 0)),
                  pl.BlockSpec((None, nblk, WIDTH, tq), lambda bi, i: (bi, 0, 0, 0))],
        out_specs=pl.BlockSpec((None, tq, WIDTH), lambda bi, i: (bi, i, 0)),
        out_shape=jax.ShapeDtypeStruct((b, t, WIDTH), BF16),
        scratch_shapes=[pltpu.VMEM((N_PAIRS, 2 * tq, MXU_TILE), BF16), pltpu.VMEM((N_PAIRS, 2 * tq, PAIR), F32),
                        pltpu.VMEM((N_PAIRS, 2 * tq, PAIR), F32), pltpu.VMEM((N_PAIRS, 2 * tq, PAIR), F32)],
        compiler_params=_cparams(("parallel", "arbitrary"), 40 << 20),
        name="mla_attn",
    )(qn, qr, kcat, vb)


def _layer_norm(y, g, b):
    mu = jnp.mean(y, axis=-1, keepdims=True)
    d = y - mu
    var = jnp.mean(d * d, axis=-1, keepdims=True)
    return d * lax.rsqrt(var + NORM_EPS) * g + b


def _silu(g):
    return g / (1.0 + jnp.exp(-g))


def _finish0_kernel(x_ref, oa_ref, ob_ref, gate_ref, w_ref, g_ref, b_ref, h_ref):
    sg = _silu(gate_ref[...])
    y = (_dot((oa_ref[...].astype(F32) * sg[:, :WIDTH]).astype(BF16), w_ref[:WIDTH, :])
         + _dot((ob_ref[...].astype(F32) * sg[:, WIDTH:]).astype(BF16), w_ref[WIDTH:, :]))
    h_ref[...] = _layer_norm(DN_ALPHA * x_ref[...] + y, g_ref[...], b_ref[...])


def _finish0(x2, oa, ob, gate, w, g, b, tm):
    m = x2.shape[0]
    row = lambda n: pl.BlockSpec((tm, n), lambda i: (i, 0))
    full = lambda a: pl.BlockSpec(a.shape, lambda i: (0,) * a.ndim)
    return pl.pallas_call(
        _finish0_kernel,
        grid=(m // tm,),
        in_specs=[row(D_MODEL), row(WIDTH), row(WIDTH), row(2 * WIDTH), full(w), full(g), full(b)],
        out_specs=row(D_MODEL),
        out_shape=jax.ShapeDtypeStruct((m, D_MODEL), F32),
        compiler_params=_cparams(("parallel",), 40 << 20),
        name="finish0",
    )(x2, oa, ob, gate, w, g, b)


def _finish1_kernel(*refs, n_groups):
    x_ref = refs[0]
    o_refs = refs[1:1 + n_groups]
    l_refs = refs[1 + n_groups:1 + 2 * n_groups] if n_groups > 1 else ()
    gate_ref, w_ref, g_ref, b_ref, h_ref = refs[-5:]
    if n_groups > 1:
        lses = [r[...] for r in l_refs]
        mx = functools.reduce(jnp.maximum, lses)
        es = [jnp.exp(l - mx) for l in lses]
        num = functools.reduce(lambda a, c: a + c, [e * r[...] for e, r in zip(es, o_refs)])
        o = num / functools.reduce(lambda a, c: a + c, es)
    else:
        o = o_refs[0][...]
    y = _dot((o * _silu(gate_ref[...])).astype(BF16), w_ref[...])
    h_ref[...] = _layer_norm(DN_ALPHA * x_ref[...] + y, g_ref[...], b_ref[...])


def _finish1_streams_kernel(x_ref, *refs, tm):
    o_refs, l_refs = refs[0:C_GROUPS], refs[C_GROUPS:2 * C_GROUPS]
    gate_ref, w_ref, g_ref, b_ref, h_ref, tok_ref = refs[2 * C_GROUPS:]
    n_slabs = WIDTH // LANES

    def to_tokens(ref, r, slot):
        if r == 1:
            return ref[0].astype(F32)
        for rho in range(r):
            for c in range(n_slabs):
                tok_ref[slot, c, pl.ds(rho, tm // r, stride=r), :] = (
                    ref[rho, :, c * LANES:(c + 1) * LANES].astype(F32))
        return jnp.concatenate([tok_ref[slot, c] for c in range(n_slabs)], axis=1)

    lses = [to_tokens(l_refs[g], r, 2 * g) for g, r in enumerate(C_DILATIONS)]
    mx = functools.reduce(jnp.maximum, lses)
    es = [jnp.exp(l - mx) for l in lses]
    den = functools.reduce(lambda a, c: a + c, es)
    num = functools.reduce(lambda a, c: a + c,
                           [e * to_tokens(o_refs[g], r, 2 * g + 1) for g, (e, r) in enumerate(zip(es, C_DILATIONS))])
    y = _dot((num / den * _silu(gate_ref[...])).astype(BF16), w_ref[...])
    h_ref[...] = _layer_norm(DN_ALPHA * x_ref[...] + y, g_ref[...], b_ref[...])


def _finish1_streams(x3, os_, lses, gate3, w, g, b, tm):
    bsz, t, _ = x3.shape
    tok = lambda n: pl.BlockSpec((None, tm, n), lambda bi, ti: (bi, ti, 0))
    stream = lambda r: pl.BlockSpec((None, r, tm // r, WIDTH), lambda bi, ti: (bi, 0, ti, 0))
    full = lambda a: pl.BlockSpec(a.shape, lambda bi, ti: (0,) * a.ndim)
    return pl.pallas_call(
        functools.partial(_finish1_streams_kernel, tm=tm),
        grid=(bsz, t // tm),
        in_specs=([tok(D_MODEL)] + [stream(r) for r in C_DILATIONS] * 2 + [tok(WIDTH), full(w), full(g), full(b)]),
        out_specs=tok(D_MODEL),
        out_shape=jax.ShapeDtypeStruct((bsz, t, D_MODEL), F32),
        scratch_shapes=[pltpu.VMEM((2 * C_GROUPS, WIDTH // LANES, tm, LANES), F32)],
        compiler_params=_cparams(("parallel", "parallel"), 40 << 20),
        name="finish1_streams",
    )(x3, *os_, *lses, gate3, w, g, b)


def _finish1(x2, os_, lses, gate, w, g, b, tm):
    m = x2.shape[0]
    n_groups = len(os_)
    row = lambda n: pl.BlockSpec((tm, n), lambda i: (i, 0))
    full = lambda a: pl.BlockSpec(a.shape, lambda i: (0,) * a.ndim)
    args = [x2, *os_, *lses, gate, w, g, b]
    in_specs = ([row(D_MODEL)] + [row(WIDTH)] * (len(os_) + len(lses)) + [row(WIDTH), full(w), full(g), full(b)])
    return pl.pallas_call(
        functools.partial(_finish1_kernel, n_groups=n_groups),
        grid=(m // tm,),
        in_specs=in_specs,
        out_specs=row(D_MODEL),
        out_shape=jax.ShapeDtypeStruct((m, D_MODEL), F32),
        compiler_params=_cparams(("parallel",), 40 << 20),
        name="finish1",
    )(*args)


def _l1_proj_kernel(x_ref, w_ref, *out_refs, kv_dtype):
    xb = x_ref[...].astype(BF16)
    q_refs = out_refs[0:C_GROUPS]
    kv_refs = out_refs[C_GROUPS:2 * C_GROUPS]
    gate_ref = out_refs[2 * C_GROUPS]
    for g in range(C_GROUPS):
        base = g * 3 * WIDTH
        q_refs[g][...] = (_dot(xb, w_ref[:, base:base + WIDTH]) * ATTN_SCALE).astype(BF16)
        kv_refs[g][...] = _dot(xb, w_ref[:, base + WIDTH:base + 3 * WIDTH]).astype(kv_dtype)
    gate_ref[...] = _dot(xb, w_ref[:, C_GROUPS * 3 * WIDTH:])


def _l1_proj(x2, w1, tm, kv_dtype):
    m = x2.shape[0]
    row = lambda n: pl.BlockSpec((tm, n), lambda i: (i, 0))
    outs = [(WIDTH, BF16)] * C_GROUPS + [(2 * WIDTH, kv_dtype)] * C_GROUPS + [(WIDTH, F32)]
    return pl.pallas_call(
        functools.partial(_l1_proj_kernel, kv_dtype=kv_dtype),
        grid=(m // tm,),
        in_specs=[row(D_MODEL),
                  pl.BlockSpec(w1.shape, lambda i: (0, 0), pipeline_mode=pl.Buffered(1))],
        out_specs=[row(n) for n, _ in outs],
        out_shape=[jax.ShapeDtypeStruct((m, n), dt) for n, dt in outs],
        compiler_params=_cparams(("parallel",), 48 << 20),
        name="l1_proj",
    )(x2, w1)


def _stream_rows(slab_ref, r, tm):
    n_slabs = slab_ref.shape[0]
    return jnp.concatenate(
        [jnp.concatenate([slab_ref[c, pl.ds(rho, tm // r, stride=r), :] for c in range(n_slabs)], axis=1)
         for rho in range(r)], axis=0)


def _l1_proj_streams_kernel(x_ref, w_ref, *refs, tm):
    q_refs, kv_refs = refs[0:C_GROUPS], refs[C_GROUPS:2 * C_GROUPS]
    gate_ref, xs_ref = refs[2 * C_GROUPS], refs[2 * C_GROUPS + 1]
    for c in range(D_MODEL // LANES):
        xs_ref[c] = x_ref[:, c * LANES:(c + 1) * LANES]
    xb = x_ref[...].astype(BF16)
    gate_ref[...] = _dot(xb, w_ref[:, C_GROUPS * 3 * WIDTH:])
    for g in range(C_GROUPS):
        r = C_DILATIONS[g]
        rows = tm // r
        xg = xb if r == 1 else _stream_rows(xs_ref, r, tm).astype(BF16)
        base = g * 3 * WIDTH
        q = (_dot(xg, w_ref[:, base:base + WIDTH]) * ATTN_SCALE).astype(BF16)
        kv = _dot(xg, w_ref[:, base + WIDTH:base + 3 * WIDTH]).astype(BF16)
        for rho in range(r):
            q_refs[g][rho] = q[rho * rows:(rho + 1) * rows]
            kv_refs[g][rho] = kv[rho * rows:(rho + 1) * rows]


def _l1_proj_streams(x3, w1, tm):
    b, t, _ = x3.shape
    s = jax.ShapeDtypeStruct
    stream = lambda r, n: pl.BlockSpec((None, r, tm // r, n), lambda bi, ti: (bi, 0, ti, 0))
    out_specs = ([stream(r, WIDTH) for r in C_DILATIONS] + [stream(r, 2 * WIDTH) for r in C_DILATIONS]
                 + [pl.BlockSpec((None, tm, WIDTH), lambda bi, ti: (bi, ti, 0))])
    out_shape = ([s((b, r, t // r, WIDTH), BF16) for r in C_DILATIONS]
                 + [s((b, r, t // r, 2 * WIDTH), BF16) for r in C_DILATIONS] + [s((b, t, WIDTH), F32)])
    return pl.pallas_call(
        functools.partial(_l1_proj_streams_kernel, tm=tm),
        grid=(b, t // tm),
        in_specs=[pl.BlockSpec((None, tm, D_MODEL), lambda bi, ti: (bi, ti, 0)),
                  pl.BlockSpec(w1.shape, lambda bi, ti: (0, 0), pipeline_mode=pl.Buffered(1))],
        out_specs=out_specs,
        out_shape=out_shape,
        scratch_shapes=[pltpu.VMEM((D_MODEL // LANES, tm, LANES), F32)],
        compiler_params=_cparams(("parallel", "parallel"), 48 << 20),
        name="l1_proj_streams",
    )(x3, w1)


def _proj_t_kernel(x_ref, w_ref, o_ref):
    o_ref[...] = _dot_t(w_ref[...], x_ref[...].astype(BF16))


def _proj_t(x3, w_t, window, tm):
    b, t, k = x3.shape
    f = w_t.shape[0]
    first = (t - window) // tm
    return pl.pallas_call(
        _proj_t_kernel,
        grid=(b, window // tm),
        in_specs=[pl.BlockSpec((None, tm, k), lambda bi, ti: (bi, first + ti, 0)),
                  pl.BlockSpec(w_t.shape, lambda bi, ti: (0, 0))],
        out_specs=pl.BlockSpec((None, f, tm), lambda bi, ti: (bi, 0, ti)),
        out_shape=jax.ShapeDtypeStruct((b, f, window), F32),
        compiler_params=_cparams(("parallel", "parallel"), 32 << 20),
        name="proj_t",
    )(x3, w_t)


def _alibi_slope(group, head):
    n = C_GROUPS * N_HEADS
    return float(2.0 ** (-8.0 * (group * N_HEADS + head + 1) / n))


def _dil_attn_kernel(q_ref, kv_ref, o_ref, lse_ref, *, group, n_blocks):
    r = C_DILATIONS[group]
    blk = C_STEPS

    def band(kn):
        qq = _iota((2 * blk, kn), 0) & (blk - 1)
        steps = qq - _iota((2 * blk, kn), 1) + (kn - blk)
        return (steps >= 0) & (steps <= C_STEPS), (r * steps).astype(F32)

    def block(si, n, first):
        qs = pl.multiple_of(n * blk, blk)
        ks = 0 if first else pl.multiple_of((n - 1) * blk, blk)
        kn = blk if first else 2 * blk
        valid, stepsf = band(kn)
        odd_row = _iota((2 * blk, 1), 0) >= blk
        even = _pair_half_mask(blk, 0)
        for p in range(N_PAIRS):
            qpair = q_ref[si, pl.ds(qs, blk), p * PAIR:(p + 1) * PAIR]
            kb = kv_ref[si, pl.ds(ks, kn), p * PAIR:(p + 1) * PAIR]
            vb = kv_ref[si, pl.ds(ks, kn), WIDTH + p * PAIR:WIDTH + (p + 1) * PAIR]
            q2 = jnp.concatenate([_keep(_pair_half_mask(blk, 0), qpair), _keep(_pair_half_mask(blk, 1), qpair)], axis=0)
            slope = jnp.where(odd_row, _alibi_slope(group, 2 * p + 1), _alibi_slope(group, 2 * p))
            s = jnp.where(valid, _dot_t(q2, kb) - slope * stepsf, NEG_INF)
            mx = jnp.max(s, axis=1, keepdims=True)
            pr = jnp.exp(s - mx)
            den = jnp.sum(pr, axis=1, keepdims=True)
            o2 = _dot(pr.astype(BF16), vb) / den
            lse2 = jnp.broadcast_to(mx + jnp.log(den), (2 * blk, PAIR))
            o_ref[si, pl.ds(qs, blk), p * PAIR:(p + 1) * PAIR] = jnp.where(even, o2[:blk], o2[blk:]).astype(BF16)
            lse_ref[si, pl.ds(qs, blk), p * PAIR:(p + 1) * PAIR] = jnp.where(even, lse2[:blk], lse2[blk:])

    def stream(si, c):
        block(si, 0, True)
        if n_blocks > 1:
            def body(n, cc):
                block(si, n, False)
                return cc
            lax.fori_loop(1, n_blocks, body, 0)
        return c

    if r == 1:
        stream(0, 0)
    else:
        lax.fori_loop(0, r, stream, 0)


def _dil_attn(q, kvb, group):
    b, r, ls, _ = q.shape
    spec = lambda n: pl.BlockSpec((None, r, ls, n), lambda bi: (bi, 0, 0, 0))
    return pl.pallas_call(
        functools.partial(_dil_attn_kernel, group=group, n_blocks=ls // C_STEPS),
        grid=(b,),
        in_specs=[spec(WIDTH), spec(2 * WIDTH)],
        out_specs=[spec(WIDTH)] * 2,
        out_shape=[jax.ShapeDtypeStruct((b, r, ls, WIDTH), BF16), jax.ShapeDtypeStruct((b, r, ls, WIDTH), F32)],
        compiler_params=_cparams(("parallel",), 48 << 20),
        name=f"dil_attn{group}",
    )(q, kvb)


def _block_diag_rows(row, width):
    sel = _head_of_lane((DEC_ROWS, width), 1) == _iota((DEC_ROWS, width), 0)
    return jnp.where(sel, jnp.broadcast_to(row.astype(F32), (DEC_ROWS, width)), 0.0).astype(row.dtype)


def _diag_heads(full):
    sel = _head_of_lane(full.shape, 1) == _iota(full.shape, 0)
    return jnp.sum(jnp.where(sel, full, 0.0), axis=0, keepdims=True)


def _sb_decode_kernel(pt_ref, q_ref, u_ref, cache_ref, o_ref, buf_ref, sem_ref, acc_ref, carry_ref, *, n_pages):
    b = pl.program_id(0)

    def page_copy(p, slot, seq=b):
        return pltpu.make_async_copy(cache_ref.at[pt_ref[seq, p]], buf_ref.at[slot], sem_ref.at[slot])

    def start_latest(seq):
        page_copy(n_pages - 1, (n_pages - 1) % 2, seq).start()
        if n_pages > 1:
            page_copy(n_pages - 2, (n_pages - 2) % 2, seq).start()

    @pl.when(b == 0)
    def _():
        start_latest(b)

    acc_ref[...] = jnp.zeros_like(acc_ref)
    carry_ref[...] = jnp.zeros_like(carry_ref)
    qbd = _block_diag_rows(q_ref[...], WIDTH)
    u = u_ref[...]

    def page_step(p, slot):
        page_copy(p, slot).wait()
        kt = buf_ref[slot, 0].astype(BF16)
        vt = buf_ref[slot, 1].astype(BF16)
        z = _dot(qbd, kt)
        sp = _softplus(z)
        lk = -sp
        hi, lo = _split_bf16(lk)
        tail = _dot(hi, u) + _dot(lo, u)
        w = jnp.exp(z - sp + tail + carry_ref[...])
        acc_ref[...] += _dot_t(w.astype(BF16), vt)
        carry_ref[...] += jnp.sum(lk, axis=1, keepdims=True)
        return jnp.max(carry_ref[:N_HEADS, :])

    page_step(n_pages - 1, (n_pages - 1) % 2)
    mx2 = page_step(n_pages - 2, (n_pages - 2) % 2)
    deeper = (mx2 >= SB_EXIT) & (n_pages > 2)

    @pl.when(deeper)
    def _():
        page_copy(n_pages - 3, (n_pages - 3) % 2).start()
        if n_pages > 3:
            page_copy(n_pages - 4, (n_pages - 4) % 2).start()

    def cond(c):
        p, mx = c
        return (p >= 0) & (mx >= SB_EXIT)

    def body(c):
        p, _ = c
        slot = lax.rem(p, 2)
        mx = page_step(p, slot)

        @pl.when((mx >= SB_EXIT) & (p >= 2))
        def _():
            page_copy(p - 2, slot).start()

        return p - 1, mx

    p_end, _ = lax.while_loop(cond, body, (jnp.int32(n_pages - 3), mx2))

    @pl.when(deeper & (p_end >= 0))
    def _():
        page_copy(p_end, lax.rem(p_end, 2)).wait()

    @pl.when(b + 1 < pl.num_programs(0))
    def _():
        start_latest(b + 1)

    o_ref[...] = _diag_heads(acc_ref[...])


def _sb_decode(page_table, cache_t, qa_s, umat):
    db, n_pages = page_table.shape
    assert n_pages >= 4
    page_shape = cache_t.shape[1:]
    grid_spec = pltpu.PrefetchScalarGridSpec(
        num_scalar_prefetch=1,
        grid=(db,),
        in_specs=[pl.BlockSpec((None, 1, WIDTH), lambda b, pt: (b, 0, 0)),
                  pl.BlockSpec(umat.shape, lambda b, pt: (0, 0)),
                  pl.BlockSpec(memory_space=pl.ANY)],
        out_specs=pl.BlockSpec((None, 1, WIDTH), lambda b, pt: (b, 0, 0)),
        scratch_shapes=[pltpu.VMEM((2,) + page_shape, F32), pltpu.SemaphoreType.DMA((2,)),
                        pltpu.VMEM((DEC_ROWS, WIDTH), F32), pltpu.VMEM((DEC_ROWS, 1), F32)],
    )
    return pl.pallas_call(
        functools.partial(_sb_decode_kernel, n_pages=n_pages),
        grid_spec=grid_spec,
        out_shape=jax.ShapeDtypeStruct((db, 1, WIDTH), F32),
        compiler_params=_cparams(("arbitrary",), 16 << 20),
        name="sb_decode",
    )(page_table, qa_s.reshape(db, 1, WIDTH), umat, cache_t)


def _mla_decode_kernel(pt_ref, qn_ref, qr_ref, lat_ref, wuk_ref, wuv_ref, gsel_ref, *refs, n_pages):
    page_refs = refs[:n_pages]
    o_ref, ql_ref, qrope_ref, acc_ref, m_ref, l_ref = refs[n_pages:]
    s = pl.program_id(1)

    @pl.when(s == 0)
    def _():
        ql = _dot(_block_diag_rows(qn_ref[...], WIDTH), wuk_ref[...]).astype(BF16)
        shape = (DEC_ROWS, 2 * LANES)
        lane, rowi = _iota(shape, 1), _iota(shape, 0)
        sel = (((lane >= LANES) == (rowi >= 4)) & ((lax.shift_right_logical(lane, 4) & 3) == (rowi & 3))
               & (rowi < N_HEADS))
        qsel = jnp.where(sel, jnp.broadcast_to(qr_ref[...].astype(F32), shape), 0.0).astype(BF16)
        qrope = _dot(qsel, gsel_ref[...]).astype(BF16)
        ql_ref[...] = ql
        qrope_ref[...] = qrope
        own = lat_ref[...].astype(BF16).astype(F32)
        s_own = (jnp.sum(ql.astype(F32) * own[:, :B_KV_RANK], axis=1, keepdims=True)
                 + jnp.sum(qrope.astype(F32) * own[:, B_KV_RANK:], axis=1, keepdims=True)) * MLA_SCALE
        m_ref[...] = s_own
        l_ref[...] = jnp.ones_like(l_ref)
        acc_ref[...] = jnp.broadcast_to(own[:, :B_KV_RANK], acc_ref.shape)

    ckv = jnp.concatenate([r[:B_KV_RANK, :].astype(BF16) for r in page_refs], axis=1)
    kro = jnp.concatenate([r[B_KV_RANK:, :].astype(BF16) for r in page_refs], axis=1)
    sc = (_dot(ql_ref[...], ckv) + _dot(qrope_ref[...], kro)) * MLA_SCALE
    m_old = m_ref[...]
    m_new = jnp.maximum(m_old, jnp.max(sc, axis=1, keepdims=True))
    pr = jnp.exp(sc - m_new)
    alpha = jnp.exp(m_old - m_new)
    l_ref[...] = alpha * l_ref[...] + jnp.sum(pr, axis=1, keepdims=True)
    acc_ref[...] = alpha * acc_ref[...] + _dot_t(pr.astype(BF16), ckv)
    m_ref[...] = m_new

    @pl.when(s == pl.num_programs(1) - 1)
    def _():
        o_lat = (acc_ref[...] / l_ref[...]).astype(BF16)
        o_ref[...] = _diag_heads(_dot(o_lat, wuv_ref[...]))


def _mla_decode(page_table, cache_t, qn_s, qr_s, lat_s, wuk_t, wuv, gsel, n_pages):
    db, pages_per_seq = page_table.shape
    n_steps = pages_per_seq // n_pages
    page_rows = cache_t.shape[2]

    def page_spec(k):
        return pl.BlockSpec((None, B_LATENT, page_rows), lambda b, s, pt: (pt[b, s * n_pages + k], 0, 0))

    per_b = lambda n: pl.BlockSpec((None, 1, n), lambda b, s, pt: (b, 0, 0))
    full = lambda a: pl.BlockSpec(a.shape, lambda b, s, pt: (0,) * a.ndim)
    grid_spec = pltpu.PrefetchScalarGridSpec(
        num_scalar_prefetch=1,
        grid=(db, n_steps),
        in_specs=[per_b(WIDTH), per_b(2 * LANES), per_b(B_LATENT), full(wuk_t), full(wuv), full(gsel)]
                 + [page_spec(k) for k in range(n_pages)],
        out_specs=per_b(WIDTH),
        scratch_shapes=[pltpu.VMEM((DEC_ROWS, B_KV_RANK), BF16), pltpu.VMEM((DEC_ROWS, B_ROPE), BF16),
                        pltpu.VMEM((DEC_ROWS, B_KV_RANK), F32), pltpu.VMEM((DEC_ROWS, 1), F32),
                        pltpu.VMEM((DEC_ROWS, 1), F32)],
    )
    return pl.pallas_call(
        functools.partial(_mla_decode_kernel, n_pages=n_pages),
        grid_spec=grid_spec,
        out_shape=jax.ShapeDtypeStruct((db, 1, WIDTH), F32),
        compiler_params=_cparams(("parallel", "arbitrary"), 40 << 20),
        name="mla_decode",
    )(page_table, qn_s.reshape(db, 1, WIDTH), qr_s.reshape(db, 1, 2 * LANES), lat_s.reshape(db, 1, B_LATENT),
      wuk_t, wuv, gsel, *([cache_t] * n_pages))


def _dil_decode_kernel(q_ref, kvn_ref, c0_ref, c1_ref, c2_ref, o_ref):
    caches = (c0_ref, c1_ref, c2_ref)
    head = _iota((DEC_ROWS, 1), 0).astype(F32)
    outs, lses = [], []
    for g in range(C_GROUPS):
        r = C_DILATIONS[g]
        win = caches[g].shape[-1]
        slope = jnp.exp((-8.0 * (g * N_HEADS + head + 1.0) / (C_GROUPS * N_HEADS)) * np.log(2.0))
        qbd = _block_diag_rows(q_ref[:, g * WIDTH:(g + 1) * WIDTH], WIDTH)
        kt = caches[g][0].astype(BF16)
        vt = caches[g][1].astype(BF16)
        k_new = kvn_ref[:, g * 2 * WIDTH:g * 2 * WIDTH + WIDTH].astype(BF16)
        v_new = kvn_ref[:, g * 2 * WIDTH + WIDTH:(g + 1) * 2 * WIDTH].astype(BF16).astype(F32)
        row = _iota((DEC_ROWS, win), 1)
        s_win = _dot(qbd, kt) - slope * (win - row).astype(F32)
        s_win = jnp.where((row & (r - 1)) == 0, s_win, NEG_INF)
        s_own = jnp.sum(qbd.astype(F32) * k_new.astype(F32), axis=1, keepdims=True)
        mx = jnp.maximum(jnp.max(s_win, axis=1, keepdims=True), s_own)
        p_win = jnp.exp(s_win - mx)
        p_own = jnp.exp(s_own - mx)
        den = jnp.sum(p_win, axis=1, keepdims=True) + p_own
        o = (_dot_t(p_win.astype(BF16), vt) + p_own.astype(BF16).astype(F32) * v_new) / den
        outs.append(o)
        lses.append(mx + jnp.log(den))
    mx = functools.reduce(jnp.maximum, lses)
    es = [jnp.exp(l - mx) for l in lses]
    tot = functools.reduce(lambda a, c: a + c, es)
    merged = functools.reduce(lambda a, c: a + c, [(e / tot) * o for e, o in zip(es, outs)])
    o_ref[...] = _diag_heads(merged)


def _dil_decode(q3, kvn3, caches_t):
    db = q3.shape[0]
    per_b = lambda n: pl.BlockSpec((None, 1, n), lambda b: (b, 0, 0))
    specs = []
    for g, c in enumerate(caches_t):
        assert c.shape[-1] == C_STEPS * C_DILATIONS[g]
        specs.append(pl.BlockSpec((None,) + c.shape[1:], lambda b: (b, 0, 0, 0)))
    return pl.pallas_call(
        _dil_decode_kernel,
        grid=(db,),
        in_specs=[per_b(C_GROUPS * WIDTH), per_b(C_GROUPS * 2 * WIDTH)] + specs,
        out_specs=per_b(WIDTH),
        out_shape=jax.ShapeDtypeStruct((db, 1, WIDTH), F32),
        compiler_params=_cparams(("parallel",), 48 << 20),
        name="dil_decode",
    )(q3.reshape(db, 1, C_GROUPS * WIDTH), kvn3.reshape(db, 1, C_GROUPS * 2 * WIDTH), *caches_t)


def _prep_w_in0(w_in0):
    cuts = np.cumsum(IN0_SIZES)[:-1].tolist()
    qa, ka, va, dq, dkv, kr, gate = jnp.split(w_in0, cuts, axis=1)
    kr_sw = jnp.concatenate([kr[:, B_HALF:], kr[:, :B_HALF]], axis=1)
    wq = jnp.concatenate([qa, gate, dq], axis=1).astype(BF16)
    wk = jnp.concatenate([ka, va, dkv, kr, kr_sw], axis=1).T.astype(BF16)
    return wq, wk


def _prep_w_uq(w_uq):
    w = w_uq.reshape(B_Q_RANK, N_HEADS, B_NOPE + B_ROPE)
    nope = w[:, :, :B_NOPE].reshape(B_Q_RANK, WIDTH)
    x1 = w[:, :, B_NOPE:B_NOPE + B_HALF]
    x2 = w[:, :, B_NOPE + B_HALF:]
    slabs, slabs_sw = [], []
    for s in range(2):
        a = x1[:, 4 * s:4 * s + 4].reshape(B_Q_RANK, 4 * B_HALF)
        c = x2[:, 4 * s:4 * s + 4].reshape(B_Q_RANK, 4 * B_HALF)
        slabs.append(jnp.concatenate([a, c], axis=1))
        slabs_sw.append(jnp.concatenate([c, a], axis=1))
    return jnp.concatenate([nope] + slabs + slabs_sw, axis=1).astype(BF16)


def _rope_tables(pos):
    inv = ROPE_THETA ** (-jnp.arange(0, B_ROPE, 2, dtype=F32) / B_ROPE)
    ang = pos.astype(F32)[:, None] * inv[None, :]
    cos, sin = jnp.cos(ang), jnp.sin(ang)
    tabq = jnp.concatenate([jnp.tile(cos, (1, 8)), jnp.tile(-sin, (1, 4)), jnp.tile(sin, (1, 4))], axis=1)
    tabk = jnp.concatenate([cos, cos, -sin, sin], axis=1).T
    return tabq, tabk


def _const_mats():
    tri = lambda n: jnp.asarray((np.arange(n)[:, None] > np.arange(n)[None, :]).astype(np.float32), BF16)
    emat = np.zeros((B_ROPE, LANES), np.float32)
    for c in range(LANES):
        emat[(c % B_HALF) + (B_HALF if c >= 4 * B_HALF else 0), c] = 1.0
    gsel = np.zeros((2 * LANES, B_ROPE), np.float32)
    for c in range(2 * LANES):
        cc = c % LANES
        gsel[c, (cc % B_HALF) + (B_HALF if cc >= 4 * B_HALF else 0)] = 1.0
    return tri(MXU_TILE), tri(LANES), jnp.asarray(emat.T, BF16), jnp.asarray(gsel, BF16)


def _kv_rows(a_t, n, t):
    return jnp.transpose(a_t.reshape(n, 2, N_HEADS, HEAD_DIM, t), (0, 4, 1, 2, 3))


def _kv_pages(c):
    return jnp.transpose(c, (0, 2, 3, 4, 1)).reshape(c.shape[0], 2, WIDTH, c.shape[1])


def kernel(x_prompt, x_sample, cache_a_kv, cache_b_latent, cache_c0_kv, cache_c1_kv, cache_c2_kv,
           page_table, w_in0, g_cq, g_ckv, w_uq, w_uk, w_uv, w_out0, ln0_g, ln0_b,
           w_in1, w_out1, ln1_g, ln1_b):
    bsz, t_p, _ = x_prompt.shape
    db, t_s, _ = x_sample.shape
    assert t_s == 1 and t_p % MXU_TILE == 0 and cache_a_kv.shape[1] == LANES
    past_len = page_table.shape[1] * cache_a_kv.shape[1]
    m_p, m_s = bsz * t_p, db * t_s
    tm_p, tm_s = MXU_TILE, m_s

    tri_blk, tri_page, emat_t, gsel = _const_mats()
    wq0, wk0 = _prep_w_in0(w_in0)
    wuq = _prep_w_uq(w_uq)
    wuk_b, wuv_b = w_uk.astype(BF16), w_uv.astype(BF16)
    wo0, wo1, w1 = w_out0.astype(BF16), w_out1.astype(BF16), w_in1.astype(BF16)
    row = lambda v: v.reshape(1, -1)
    tabq_p, tabk_p = _rope_tables(jnp.arange(t_p))
    tabq_s, tabk_s = _rope_tables(jnp.full((m_s,), past_len, jnp.int32))

    xp = x_prompt.reshape(m_p, D_MODEL)
    xs3 = x_sample.reshape(1, m_s, D_MODEL)
    xs = xs3.reshape(m_s, D_MODEL)

    proj = lambda x3, tq_, tk_, tm: _l0_proj(x3, tq_, tk_, wq0, wk0, row(g_cq), g_ckv.reshape(-1, 1), wuq,
                                             wuk_b.T, wuv_b.T, emat_t, tm)
    qa_p, gate_p, qn_p, qr_p, kvf_p, lat_p, kvb_p, kcat_p, vb_p = proj(x_prompt, tabq_p, tabk_p, tm_p)
    qa_s, gate_s, qn_s, qr_s, kvf_s, lat_s, _, _, _ = proj(xs3, tabq_s, tabk_s, tm_s)

    oa_p = _sb_attn(qa_p, kvb_p, tri_blk).reshape(m_p, WIDTH)
    ob_p = _mla_attn(qn_p, qr_p, kcat_p, vb_p).reshape(m_p, WIDTH)
    h_p = _finish0(xp, oa_p, ob_p, gate_p.reshape(m_p, 2 * WIDTH), wo0, row(ln0_g), row(ln0_b), tm_p)

    lat_s_rows = lat_s[0].T
    oa_s = _sb_decode(page_table, _kv_pages(cache_a_kv), qa_s[0], tri_page).reshape(m_s, WIDTH)
    ob_s = _mla_decode(page_table, jnp.transpose(cache_b_latent, (0, 2, 1)), qn_s[0], qr_s[0], lat_s_rows,
                       wuk_b.T, wuv_b, gsel, n_pages=min(32, page_table.shape[1])).reshape(m_s, WIDTH)
    h_s = _finish0(xs, oa_s, ob_s, gate_s[0], wo0, row(ln0_g), row(ln0_b), tm_s)

    h_p3 = h_p.reshape(bsz, t_p, D_MODEL)
    outs_p = _l1_proj_streams(h_p3, w1, tm_p)
    q_p, kvb1_p, gate1_p = outs_p[0:3], outs_p[3:6], outs_p[6]
    os_p, lses_p, c_new_p = [], [], []
    for g, c in enumerate((cache_c0_kv, cache_c1_kv, cache_c2_kv)):
        o, lse = _dil_attn(q_p[g], kvb1_p[g], g)
        os_p.append(o)
        lses_p.append(lse)
        win = min(c.shape[1], t_p)
        w_kv_t = w1[:, g * 3 * WIDTH + WIDTH:(g + 1) * 3 * WIDTH].T
        c_new_p.append(_kv_rows(_proj_t(h_p3, w_kv_t, win, min(win, 2 * MXU_TILE)), bsz, win))
    y_p = _finish1_streams(h_p3, os_p, lses_p, gate1_p, wo1, row(ln1_g), row(ln1_b), tm_p)

    outs_s = _l1_proj(h_s, w1, tm_s, F32)
    q_s, kvf1_s, gate1_s = outs_s[0:3], outs_s[3:6], outs_s[6]
    oc_s = _dil_decode(jnp.concatenate(q_s, axis=1), jnp.concatenate(kvf1_s, axis=1),
                       [_kv_pages(c) for c in (cache_c0_kv, cache_c1_kv, cache_c2_kv)]).reshape(m_s, WIDTH)
    y_s = _finish1(h_s, [oc_s], [], gate1_s, wo1, row(ln1_g), row(ln1_b), tm_s)

    kv5 = lambda a: a.reshape(db, t_s, 2, N_HEADS, HEAD_DIM)
    return (y_p.reshape(bsz, t_p, D_MODEL), y_s.reshape(db, t_s, D_MODEL),
            _kv_rows(kvf_p, bsz, t_p), kv5(kvf_s[0].T),
            jnp.transpose(lat_p, (0, 2, 1)), lat_s_rows.reshape(db, t_s, B_LATENT),
            c_new_p[0], kv5(kvf1_s[0]), c_new_p[1], kv5(kvf1_s[1]), c_new_p[2], kv5(kvf1_s[2]))
```

```python
import functools

import numpy as np
import jax
import jax.numpy as jnp
from jax import lax
from jax.experimental import pallas as pl
from jax.experimental.pallas import tpu as pltpu

D_MODEL = 1024
DEPTH = 2
HEAD_DIM = 64
N_HEADS = 8
WIDTH = N_HEADS * HEAD_DIM
ATTN_SCALE = HEAD_DIM ** -0.5
B_NOPE = 64
B_ROPE = 32
B_HALF = B_ROPE // 2
B_Q_RANK = 384
B_KV_RANK = 256
B_LATENT = B_KV_RANK + B_ROPE
MLA_SCALE = (B_NOPE + B_ROPE) ** -0.5
ROPE_THETA = 10000.0
C_DILATIONS = (1, 4, 16)
C_GROUPS = 3
C_STEPS = 128
NORM_EPS = 1e-5
DN_ALPHA = (2 * DEPTH) ** 0.25
IN0_SIZES = (WIDTH, WIDTH, WIDTH, B_Q_RANK, B_KV_RANK, B_ROPE, 2 * WIDTH)

LANES = 128
PAIR = 2 * HEAD_DIM
N_PAIRS = N_HEADS // 2
V7X_VMEM_BYTES = 64 * 1024 * 1024
MXU_TILE = 256
DEC_ROWS = 16

NEG_INF = float("-inf")
SB_EXIT = -104.0
F32 = jnp.float32
BF16 = jnp.bfloat16


def _cparams(semantics, vmem_bytes):
    return pltpu.CompilerParams(dimension_semantics=semantics,
                                vmem_limit_bytes=min(int(vmem_bytes), V7X_VMEM_BYTES - (4 << 20)))


def _dot(a, b):
    return jnp.dot(a, b, preferred_element_type=F32)


def _dot_t(a, b):
    return lax.dot_general(a, b, (((1,), (1,)), ((), ())), preferred_element_type=F32)


def _iota(shape, dim):
    return lax.broadcasted_iota(jnp.int32, shape, dim)


def _softplus(z):
    return jnp.maximum(z, 0.0) + jnp.log(1.0 + jnp.exp(-jnp.abs(z)))


def _split_bf16(x):
    hi = x.astype(BF16)
    lo = (x - hi.astype(F32)).astype(BF16)
    return hi, lo


def _pair_half_mask(rows, head):
    lane = _iota((rows, PAIR), 1)
    return (lane >= HEAD_DIM) if head % 2 else (lane < HEAD_DIM)


def _keep(mask, x):
    return jnp.where(mask, x.astype(F32), 0.0).astype(x.dtype)


def _head_of_lane(shape, dim):
    return lax.shift_right_logical(_iota(shape, dim), 6)


def _twice(x):
    return jnp.concatenate([x, x], axis=1)


_Q_QA, _Q_GATE, _Q_DQ, _N_Q0 = 0, WIDTH, 3 * WIDTH, 3 * WIDTH + B_Q_RANK
_K_KV, _K_DKV, _K_KR, _N_K0 = 0, 2 * WIDTH, 2 * WIDTH + B_KV_RANK, 2 * WIDTH + B_KV_RANK + 2 * B_ROPE
_N_UQ = WIDTH + 4 * LANES


def _l0_proj_kernel(x_ref, tabq_ref, tabk_ref, wq_ref, wk_ref, gcq_ref, gckv_ref, wuq_ref, wuk_ref, wuv_ref,
                    e_ref, qa_ref, gate_ref, qn_ref, qr_ref, kvf_ref, lat_ref, kvb_ref, kcat_ref, vb_ref):
    xb = x_ref[...].astype(BF16)
    qa_ref[...] = (_dot(xb, wq_ref[:, _Q_QA:_Q_GATE]) * ATTN_SCALE).astype(BF16)
    gate_ref[...] = _dot(xb, wq_ref[:, _Q_GATE:_Q_DQ])
    dq = _dot(xb, wq_ref[:, _Q_DQ:_N_Q0])
    dqn = dq * lax.rsqrt(jnp.mean(dq * dq, axis=-1, keepdims=True) + NORM_EPS) * gcq_ref[...]
    qf = _dot(dqn.astype(BF16), wuq_ref[...])
    qn_ref[...] = qf[:, :WIDTH].astype(BF16)
    cosq = tabq_ref[:, :LANES]
    sinq = tabq_ref[:, LANES:]
    for s in range(2):
        nat = qf[:, WIDTH + s * LANES:WIDTH + (s + 1) * LANES]
        swp = qf[:, WIDTH + (2 + s) * LANES:WIDTH + (3 + s) * LANES]
        qr_ref[:, s * LANES:(s + 1) * LANES] = (nat * cosq + swp * sinq).astype(BF16)
    kv = _dot_t(wk_ref[_K_KV:_K_DKV, :], xb)
    kvf_ref[...] = kv
    kvb_ref[...] = kv.astype(BF16)
    dkv = _dot_t(wk_ref[_K_DKV:_K_KR, :], xb)
    ckv = dkv * lax.rsqrt(jnp.mean(dkv * dkv, axis=0, keepdims=True) + NORM_EPS) * gckv_ref[...]
    lat_ref[:B_KV_RANK, :] = ckv
    kr2 = _dot_t(wk_ref[_K_KR:_N_K0, :], xb)
    krot = kr2[:B_ROPE, :] * tabk_ref[:B_ROPE, :] + kr2[B_ROPE:, :] * tabk_ref[B_ROPE:, :]
    lat_ref[B_KV_RANK:, :] = krot
    ckv_b = ckv.astype(BF16)
    kn = _dot(wuk_ref[...], ckv_b).astype(BF16)
    vb_ref[...] = _dot(wuv_ref[...], ckv_b).astype(BF16)
    kslab = _dot(e_ref[...], krot.astype(BF16)).astype(BF16)
    for p in range(N_PAIRS):
        kcat_ref[p * MXU_TILE:p * MXU_TILE + PAIR, :] = kn[p * PAIR:(p + 1) * PAIR, :]
        kcat_ref[p * MXU_TILE + PAIR:(p + 1) * MXU_TILE, :] = kslab


def _l0_proj(x3, tabq, tabk, wq, wk, gcq, gckv, wuq, wuk_t, wuv_t, emat_t, tm):
    b, t, _ = x3.shape
    nt = t // tm
    tok = lambda n: pl.BlockSpec((None, tm, n), lambda bi, ti: (bi, ti, 0))
    feat = lambda n: pl.BlockSpec((None, n, tm), lambda bi, ti: (bi, 0, ti))
    featb = lambda n: pl.BlockSpec((None, None, n, tm), lambda bi, ti: (bi, ti, 0, 0))
    full = lambda a: pl.BlockSpec(a.shape, lambda bi, ti: (0,) * a.ndim, pipeline_mode=pl.Buffered(1))
    s = jax.ShapeDtypeStruct
    return pl.pallas_call(
        _l0_proj_kernel,
        grid=(b, nt),
        in_specs=[tok(D_MODEL), pl.BlockSpec((tm, 2 * LANES), lambda bi, ti: (ti, 0)),
                  pl.BlockSpec((2 * B_ROPE, tm), lambda bi, ti: (0, ti)),
                  full(wq), full(wk), full(gcq), full(gckv), full(wuq), full(wuk_t), full(wuv_t), full(emat_t)],
        out_specs=[tok(WIDTH), tok(2 * WIDTH), tok(WIDTH), tok(2 * LANES),
                   feat(2 * WIDTH), feat(B_LATENT),
                   featb(2 * WIDTH), featb(N_PAIRS * MXU_TILE), featb(WIDTH)],
        out_shape=[s((b, t, WIDTH), BF16), s((b, t, 2 * WIDTH), F32), s((b, t, WIDTH), BF16),
                   s((b, t, 2 * LANES), BF16),
                   s((b, 2 * WIDTH, t), F32), s((b, B_LATENT, t), F32),
                   s((b, nt, 2 * WIDTH, tm), BF16), s((b, nt, N_PAIRS * MXU_TILE, tm), BF16),
                   s((b, nt, WIDTH, tm), BF16)],
        compiler_params=_cparams(("parallel", "parallel"), 48 << 20),
        name="l0_proj",
    )(x3, tabq, tabk, wq, wk, gcq, gckv, wuq, wuk_t, wuv_t, emat_t)


def _sb_attn_kernel(q_ref, kv_ref, u_ref, o_ref, qp_ref, acc_ref, carry_ref, *, tq):
    i = pl.program_id(1)
    acc_ref[...] = jnp.zeros_like(acc_ref)
    carry_ref[...] = jnp.zeros_like(carry_ref)
    for h in range(N_HEADS):
        qp_ref[h // 2, (h % 2) * tq:(h % 2 + 1) * tq, :] = _keep(
            _pair_half_mask(tq, h), q_ref[:, (h // 2) * PAIR:(h // 2 + 1) * PAIR])

    def block(j, masked):
        u = u_ref[...]
        if masked:
            strict = _iota((2 * tq, tq), 1) < (_iota((2 * tq, tq), 0) & (tq - 1))
        for p in range(N_PAIRS):
            kt = kv_ref[j, p * PAIR:(p + 1) * PAIR, :]
            vt = kv_ref[j, WIDTH + p * PAIR:WIDTH + (p + 1) * PAIR, :]
            z = _dot(qp_ref[p], kt)
            sp = _softplus(z)
            lk = jnp.where(strict, -sp, 0.0) if masked else -sp
            hi, lo = _split_bf16(lk)
            tail = _dot(jnp.concatenate([hi, lo], axis=1), u)
            w = jnp.exp(z - sp + tail + _twice(carry_ref[p]))
            if masked:
                w = jnp.where(strict, w, 0.0)
            acc_ref[p] += _dot_t(w.astype(BF16), vt)
            carry_ref[p] += jnp.sum(lk, axis=1, keepdims=True)

    def least_decayed():
        m = carry_ref[0]
        for p in range(1, N_PAIRS):
            m = jnp.maximum(m, carry_ref[p])
        return jnp.max(m)

    block(i, True)

    def cond(c):
        jj, mx = c
        return (jj <= i) & (mx >= SB_EXIT)

    def body(c):
        jj, _ = c
        block(i - jj, False)
        return jj + 1, least_decayed()

    lax.while_loop(cond, body, (jnp.int32(1), least_decayed()))
    for p in range(N_PAIRS):
        o_ref[:, p * PAIR:(p + 1) * PAIR] = jnp.where(
            _pair_half_mask(tq, 0), acc_ref[p, :tq, :], acc_ref[p, tq:, :]).astype(o_ref.dtype)


def _sb_attn(qa, kvb, umat):
    b, t, _ = qa.shape
    _, nblk, _, tq = kvb.shape
    return pl.pallas_call(
        functools.partial(_sb_attn_kernel, tq=tq),
        grid=(b, nblk),
        in_specs=[pl.BlockSpec((None, tq, WIDTH), lambda bi, i: (bi, i, 0)),
                  pl.BlockSpec((None, nblk, 2 * WIDTH, tq), lambda bi, i: (bi, 0, 0, 0)),
                  pl.BlockSpec(umat.shape, lambda bi, i: (0, 0))],
        out_specs=pl.BlockSpec((None, tq, WIDTH), lambda bi, i: (bi, i, 0)),
        out_shape=jax.ShapeDtypeStruct((b, t, WIDTH), BF16),
        scratch_shapes=[pltpu.VMEM((N_PAIRS, 2 * tq, PAIR), BF16), pltpu.VMEM((N_PAIRS, 2 * tq, PAIR), F32),
                        pltpu.VMEM((N_PAIRS, 2 * tq, PAIR), F32)],
        compiler_params=_cparams(("parallel", "arbitrary"), 40 << 20),
        name="sb_attn",
    )(qa, kvb, umat)


_MLA_EXP2 = MLA_SCALE * float(np.log2(np.e))


def _rope_slab_mask(rows, head):
    lane = _iota((rows, LANES), 1)
    return (lax.shift_right_logical(lane, 4) & 3) == (head % 4)


def _mla_attn_kernel(qn_ref, qr_ref, kcat_ref, vb_ref, o_ref, qc_ref, acc_ref, m_ref, l_ref, *, tq):
    i = pl.program_id(1)
    acc_ref[...] = jnp.zeros_like(acc_ref)
    m_ref[...] = jnp.full_like(m_ref, NEG_INF)
    l_ref[...] = jnp.zeros_like(l_ref)
    for h in range(N_HEADS):
        rows = slice((h % 2) * tq, (h % 2 + 1) * tq)
        qc_ref[h // 2, rows, :PAIR] = _keep(_pair_half_mask(tq, h), qn_ref[:, (h // 2) * PAIR:(h // 2 + 1) * PAIR])
        qc_ref[h // 2, rows, PAIR:] = _keep(_rope_slab_mask(tq, h), qr_ref[:, (h // 4) * LANES:(h // 4 + 1) * LANES])

    def block(j, masked):
        if masked:
            causal = _iota((2 * tq, tq), 1) <= (_iota((2 * tq, tq), 0) & (tq - 1))
        for p in range(N_PAIRS):
            kt = kcat_ref[j, p * MXU_TILE:(p + 1) * MXU_TILE, :]
            vt = vb_ref[j, p * PAIR:(p + 1) * PAIR, :]
            s = _dot(qc_ref[p], kt)
            if masked:
                s = jnp.where(causal, s, NEG_INF)
            m_old = m_ref[p]
            m_new = jnp.maximum(m_old, jnp.max(s, axis=1, keepdims=True))
            pr = jnp.exp2((s - _twice(m_new)) * _MLA_EXP2)
            alpha = jnp.exp2((m_old - m_new) * _MLA_EXP2)
            od = _dot_t(pr.astype(BF16), jnp.concatenate([vt, jnp.ones((PAIR, tq), BF16)], axis=0))
            l_ref[p] = alpha * l_ref[p] + od[:, PAIR:]
            acc_ref[p] = alpha * acc_ref[p] + od[:, :PAIR]
            m_ref[p] = m_new

    block(i, True)

    def body(j, c):
        block(j, False)
        return c

    lax.fori_loop(0, i, body, 0)
    for p in range(N_PAIRS):
        o_ref[:, p * PAIR:(p + 1) * PAIR] = jnp.where(
            _pair_half_mask(tq, 0), acc_ref[p, :tq, :] / l_ref[p, :tq, :],
            acc_ref[p, tq:, :] / l_ref[p, tq:, :]).astype(o_ref.dtype)


def _mla_attn(qn, qr, kcat, vb):
    b, t, _ = qn.shape
    _, nblk, _, tq = kcat.shape
    return pl.pallas_call(
        functools.partial(_mla_attn_kernel, tq=tq),
        grid=(b, nblk),
        in_specs=[pl.BlockSpec((None, tq, WIDTH), lambda bi, i: (bi, i, 0)),
                  pl.BlockSpec((None, tq, 2 * LANES), lambda bi, i: (bi, i, 0)),
                  pl.BlockSpec((None, nblk, N_PAIRS * MXU_TILE, tq), lambda bi, i: (bi, 0, 0, 0)),
                  pl.BlockSpec((None, nblk, WIDTH, tq), lambda bi, i: (bi, 0, 0, 0))],
        out_specs=pl.BlockSpec((None, tq, WIDTH), lambda bi, i: (bi, i, 0)),
        out_shape=jax.ShapeDtypeStruct((b, t, WIDTH), BF16),
        scratch_shapes=[pltpu.VMEM((N_PAIRS, 2 * tq, MXU_TILE), BF16), pltpu.VMEM((N_PAIRS, 2 * tq, PAIR), F32),
                        pltpu.VMEM((N_PAIRS, 2 * tq, PAIR), F32), pltpu.VMEM((N_PAIRS, 2 * tq, PAIR), F32)],
        compiler_params=_cparams(("parallel", "arbitrary"), 40 << 20),
        name="mla_attn",
    )(qn, qr, kcat, vb)


def _layer_norm(y, g, b):
    mu = jnp.mean(y, axis=-1, keepdims=True)
    d = y - mu
    var = jnp.mean(d * d, axis=-1, keepdims=True)
    return d * lax.rsqrt(var + NORM_EPS) * g + b


def _silu(g):
    return g / (1.0 + jnp.exp(-g))


def _finish0_kernel(x_ref, oa_ref, ob_ref, gate_ref, w_ref, g_ref, b_ref, h_ref):
    sg = _silu(gate_ref[...])
    y = (_dot((oa_ref[...].astype(F32) * sg[:, :WIDTH]).astype(BF16), w_ref[:WIDTH, :])
         + _dot((ob_ref[...].astype(F32) * sg[:, WIDTH:]).astype(BF16), w_ref[WIDTH:, :]))
    h_ref[...] = _layer_norm(DN_ALPHA * x_ref[...] + y, g_ref[...], b_ref[...])


def _finish0(x2, oa, ob, gate, w, g, b, tm):
    m = x2.shape[0]
    row = lambda n: pl.BlockSpec((tm, n), lambda i: (i, 0))
    full = lambda a: pl.BlockSpec(a.shape, lambda i: (0,) * a.ndim)
    return pl.pallas_call(
        _finish0_kernel,
        grid=(m // tm,),
        in_specs=[row(D_MODEL), row(WIDTH), row(WIDTH), row(2 * WIDTH), full(w), full(g), full(b)],
        out_specs=row(D_MODEL),
        out_shape=jax.ShapeDtypeStruct((m, D_MODEL), F32),
        compiler_params=_cparams(("parallel",), 40 << 20),
        name="finish0",
    )(x2, oa, ob, gate, w, g, b)


def _finish1_kernel(*refs, n_groups):
    x_ref = refs[0]
    o_refs = refs[1:1 + n_groups]
    l_refs = refs[1 + n_groups:1 + 2 * n_groups] if n_groups > 1 else ()
    gate_ref, w_ref, g_ref, b_ref, h_ref = refs[-5:]
    if n_groups > 1:
        lses = [r[...] for r in l_refs]
        mx = functools.reduce(jnp.maximum, lses)
        es = [jnp.exp(l - mx) for l in lses]
        num = functools.reduce(lambda a, c: a + c, [e * r[...] for e, r in zip(es, o_refs)])
        o = num / functools.reduce(lambda a, c: a + c, es)
    else:
        o = o_refs[0][...]
    y = _dot((o * _silu(gate_ref[...])).astype(BF16), w_ref[...])
    h_ref[...] = _layer_norm(DN_ALPHA * x_ref[...] + y, g_ref[...], b_ref[...])


def _finish1_streams_kernel(x_ref, *refs, tm):
    o_refs, l_refs = refs[0:C_GROUPS], refs[C_GROUPS:2 * C_GROUPS]
    gate_ref, w_ref, g_ref, b_ref, h_ref, tok_ref = refs[2 * C_GROUPS:]
    n_slabs = WIDTH // LANES

    def to_tokens(ref, r, slot):
        if r == 1:
            return ref[0].astype(F32)
        for rho in range(r):
            for c in range(n_slabs):
                tok_ref[slot, c, pl.ds(rho, tm // r, stride=r), :] = (
                    ref[rho, :, c * LANES:(c + 1) * LANES].astype(F32))
        return jnp.concatenate([tok_ref[slot, c] for c in range(n_slabs)], axis=1)

    lses = [to_tokens(l_refs[g], r, 2 * g) for g, r in enumerate(C_DILATIONS)]
    mx = functools.reduce(jnp.maximum, lses)
    es = [jnp.exp(l - mx) for l in lses]
    den = functools.reduce(lambda a, c: a + c, es)
    num = functools.reduce(lambda a, c: a + c,
                           [e * to_tokens(o_refs[g], r, 2 * g + 1) for g, (e, r) in enumerate(zip(es, C_DILATIONS))])
    y = _dot((num / den * _silu(gate_ref[...])).astype(BF16), w_ref[...])
    h_ref[...] = _layer_norm(DN_ALPHA * x_ref[...] + y, g_ref[...], b_ref[...])


def _finish1_streams(x3, os_, lses, gate3, w, g, b, tm):
    bsz, t, _ = x3.shape
    tok = lambda n: pl.BlockSpec((None, tm, n), lambda bi, ti: (bi, ti, 0))
    stream = lambda r: pl.BlockSpec((None, r, tm // r, WIDTH), lambda bi, ti: (bi, 0, ti, 0))
    full = lambda a: pl.BlockSpec(a.shape, lambda bi, ti: (0,) * a.ndim)
    return pl.pallas_call(
        functools.partial(_finish1_streams_kernel, tm=tm),
        grid=(bsz, t // tm),
        in_specs=([tok(D_MODEL)] + [stream(r) for r in C_DILATIONS] * 2 + [tok(WIDTH), full(w), full(g), full(b)]),
        out_specs=tok(D_MODEL),
        out_shape=jax.ShapeDtypeStruct((bsz, t, D_MODEL), F32),
        scratch_shapes=[pltpu.VMEM((2 * C_GROUPS, WIDTH // LANES, tm, LANES), F32)],
        compiler_params=_cparams(("parallel", "parallel"), 40 << 20),
        name="finish1_streams",
    )(x3, *os_, *lses, gate3, w, g, b)


def _finish1(x2, os_, lses, gate, w, g, b, tm):
    m = x2.shape[0]
    n_groups = len(os_)
    row = lambda n: pl.BlockSpec((tm, n), lambda i: (i, 0))
    full = lambda a: pl.BlockSpec(a.shape, lambda i: (0,) * a.ndim)
    args = [x2, *os_, *lses, gate, w, g, b]
    in_specs = ([row(D_MODEL)] + [row(WIDTH)] * (len(os_) + len(lses)) + [row(WIDTH), full(w), full(g), full(b)])
    return pl.pallas_call(
        functools.partial(_finish1_kernel, n_groups=n_groups),
        grid=(m // tm,),
        in_specs=in_specs,
        out_specs=row(D_MODEL),
        out_shape=jax.ShapeDtypeStruct((m, D_MODEL), F32),
        compiler_params=_cparams(("parallel",), 40 << 20),
        name="finish1",
    )(*args)


def _l1_proj_kernel(x_ref, w_ref, *out_refs, kv_dtype):
    xb = x_ref[...].astype(BF16)
    q_refs = out_refs[0:C_GROUPS]
    kv_refs = out_refs[C_GROUPS:2 * C_GROUPS]
    gate_ref = out_refs[2 * C_GROUPS]
    for g in range(C_GROUPS):
        base = g * 3 * WIDTH
        q_refs[g][...] = (_dot(xb, w_ref[:, base:base + WIDTH]) * ATTN_SCALE).astype(BF16)
        kv_refs[g][...] = _dot(xb, w_ref[:, base + WIDTH:base + 3 * WIDTH]).astype(kv_dtype)
    gate_ref[...] = _dot(xb, w_ref[:, C_GROUPS * 3 * WIDTH:])


def _l1_proj(x2, w1, tm, kv_dtype):
    m = x2.shape[0]
    row = lambda n: pl.BlockSpec((tm, n), lambda i: (i, 0))
    outs = [(WIDTH, BF16)] * C_GROUPS + [(2 * WIDTH, kv_dtype)] * C_GROUPS + [(WIDTH, F32)]
    return pl.pallas_call(
        functools.partial(_l1_proj_kernel, kv_dtype=kv_dtype),
        grid=(m // tm,),
        in_specs=[row(D_MODEL),
                  pl.BlockSpec(w1.shape, lambda i: (0, 0), pipeline_mode=pl.Buffered(1))],
        out_specs=[row(n) for n, _ in outs],
        out_shape=[jax.ShapeDtypeStruct((m, n), dt) for n, dt in outs],
        compiler_params=_cparams(("parallel",), 48 << 20),
        name="l1_proj",
    )(x2, w1)


def _stream_rows(slab_ref, r, tm):
    n_slabs = slab_ref.shape[0]
    return jnp.concatenate(
        [jnp.concatenate([slab_ref[c, pl.ds(rho, tm // r, stride=r), :] for c in range(n_slabs)], axis=1)
         for rho in range(r)], axis=0)


def _l1_proj_streams_kernel(x_ref, w_ref, *refs, tm):
    q_refs, kv_refs = refs[0:C_GROUPS], refs[C_GROUPS:2 * C_GROUPS]
    gate_ref, xs_ref = refs[2 * C_GROUPS], refs[2 * C_GROUPS + 1]
    for c in range(D_MODEL // LANES):
        xs_ref[c] = x_ref[:, c * LANES:(c + 1) * LANES]
    xb = x_ref[...].astype(BF16)
    gate_ref[...] = _dot(xb, w_ref[:, C_GROUPS * 3 * WIDTH:])
    for g in range(C_GROUPS):
        r = C_DILATIONS[g]
        rows = tm // r
        xg = xb if r == 1 else _stream_rows(xs_ref, r, tm).astype(BF16)
        base = g * 3 * WIDTH
        q = (_dot(xg, w_ref[:, base:base + WIDTH]) * ATTN_SCALE).astype(BF16)
        kv = _dot(xg, w_ref[:, base + WIDTH:base + 3 * WIDTH]).astype(BF16)
        for rho in range(r):
            q_refs[g][rho] = q[rho * rows:(rho + 1) * rows]
            kv_refs[g][rho] = kv[rho * rows:(rho + 1) * rows]


def _l1_proj_streams(x3, w1, tm):
    b, t, _ = x3.shape
    s = jax.ShapeDtypeStruct
    stream = lambda r, n: pl.BlockSpec((None, r, tm // r, n), lambda bi, ti: (bi, 0, ti, 0))
    out_specs = ([stream(r, WIDTH) for r in C_DILATIONS] + [stream(r, 2 * WIDTH) for r in C_DILATIONS]
                 + [pl.BlockSpec((None, tm, WIDTH), lambda bi, ti: (bi, ti, 0))])
    out_shape = ([s((b, r, t // r, WIDTH), BF16) for r in C_DILATIONS]
                 + [s((b, r, t // r, 2 * WIDTH), BF16) for r in C_DILATIONS] + [s((b, t, WIDTH), F32)])
    return pl.pallas_call(
        functools.partial(_l1_proj_streams_kernel, tm=tm),
        grid=(b, t // tm),
        in_specs=[pl.BlockSpec((None, tm, D_MODEL), lambda bi, ti: (bi, ti, 0)),
                  pl.BlockSpec(w1.shape, lambda bi, ti: (0, 0), pipeline_mode=pl.Buffered(1))],
        out_specs=out_specs,
        out_shape=out_shape,
        scratch_shapes=[pltpu.VMEM((D_MODEL // LANES, tm, LANES), F32)],
        compiler_params=_cparams(("parallel", "parallel"), 48 << 20),
        name="l1_proj_streams",
    )(x3, w1)


def _proj_t_kernel(x_ref, w_ref, o_ref):
    o_ref[...] = _dot_t(w_ref[...], x_ref[...].astype(BF16))


def _proj_t(x3, w_t, window, tm):
    b, t, k = x3.shape
    f = w_t.shape[0]
    first = (t - window) // tm
    return pl.pallas_call(
        _proj_t_kernel,
        grid=(b, window // tm),
        in_specs=[pl.BlockSpec((None, tm, k), lambda bi, ti: (bi, first + ti, 0)),
                  pl.BlockSpec(w_t.shape, lambda bi, ti: (0, 0))],
        out_specs=pl.BlockSpec((None, f, tm), lambda bi, ti: (bi, 0, ti)),
        out_shape=jax.ShapeDtypeStruct((b, f, window), F32),
        compiler_params=_cparams(("parallel", "parallel"), 32 << 20),
        name="proj_t",
    )(x3, w_t)


def _alibi_slope(group, head):
    n = C_GROUPS * N_HEADS
    return float(2.0 ** (-8.0 * (group * N_HEADS + head + 1) / n))


def _dil_attn_kernel(q_ref, kv_ref, o_ref, lse_ref, *, group, n_blocks):
    r = C_DILATIONS[group]
    blk = C_STEPS

    def band(kn):
        qq = _iota((2 * blk, kn), 0) & (blk - 1)
        steps = qq - _iota((2 * blk, kn), 1) + (kn - blk)
        return (steps >= 0) & (steps <= C_STEPS), (r * steps).astype(F32)

    def block(si, n, first):
        qs = pl.multiple_of(n * blk, blk)
        ks = 0 if first else pl.multiple_of((n - 1) * blk, blk)
        kn = blk if first else 2 * blk
        valid, stepsf = band(kn)
        odd_row = _iota((2 * blk, 1), 0) >= blk
        even = _pair_half_mask(blk, 0)
        for p in range(N_PAIRS):
            qpair = q_ref[si, pl.ds(qs, blk), p * PAIR:(p + 1) * PAIR]
            kb = kv_ref[si, pl.ds(ks, kn), p * PAIR:(p + 1) * PAIR]
            vb = kv_ref[si, pl.ds(ks, kn), WIDTH + p * PAIR:WIDTH + (p + 1) * PAIR]
            q2 = jnp.concatenate([_keep(_pair_half_mask(blk, 0), qpair), _keep(_pair_half_mask(blk, 1), qpair)], axis=0)
            slope = jnp.where(odd_row, _alibi_slope(group, 2 * p + 1), _alibi_slope(group, 2 * p))
            s = jnp.where(valid, _dot_t(q2, kb) - slope * stepsf, NEG_INF)
            mx = jnp.max(s, axis=1, keepdims=True)
            pr = jnp.exp(s - mx)
            if first:
                den = jnp.sum(pr, axis=1, keepdims=True)
                o2 = _dot(pr.astype(BF16), vb) / den
                lse2 = jnp.broadcast_to(mx + jnp.log(den), (2 * blk, PAIR))
            else:
                od = _dot(pr.astype(BF16), jnp.concatenate([vb, jnp.ones((kn, PAIR), BF16)], axis=1))
                den = od[:, PAIR:]
                o2 = od[:, :PAIR] / den
                lse2 = mx + jnp.log(den)
            o_ref[si, pl.ds(qs, blk), p * PAIR:(p + 1) * PAIR] = jnp.where(even, o2[:blk], o2[blk:]).astype(BF16)
            lse_ref[si, pl.ds(qs, blk), p * PAIR:(p + 1) * PAIR] = jnp.where(even, lse2[:blk], lse2[blk:])

    def stream(si, c):
        block(si, 0, True)
        if n_blocks > 1:
            def body(n, cc):
                block(si, n, False)
                return cc
            lax.fori_loop(1, n_blocks, body, 0)
        return c

    if r == 1:
        stream(0, 0)
    else:
        lax.fori_loop(0, r, stream, 0)


def _dil_attn(q, kvb, group):
    b, r, ls, _ = q.shape
    spec = lambda n: pl.BlockSpec((None, r, ls, n), lambda bi: (bi, 0, 0, 0))
    return pl.pallas_call(
        functools.partial(_dil_attn_kernel, group=group, n_blocks=ls // C_STEPS),
        grid=(b,),
        in_specs=[spec(WIDTH), spec(2 * WIDTH)],
        out_specs=[spec(WIDTH)] * 2,
        out_shape=[jax.ShapeDtypeStruct((b, r, ls, WIDTH), BF16), jax.ShapeDtypeStruct((b, r, ls, WIDTH), F32)],
        compiler_params=_cparams(("parallel",), 48 << 20),
        name=f"dil_attn{group}",
    )(q, kvb)


def _block_diag_rows(row, width):
    sel = _head_of_lane((DEC_ROWS, width), 1) == _iota((DEC_ROWS, width), 0)
    return jnp.where(sel, jnp.broadcast_to(row.astype(F32), (DEC_ROWS, width)), 0.0).astype(row.dtype)


def _diag_heads(full):
    sel = _head_of_lane(full.shape, 1) == _iota(full.shape, 0)
    return jnp.sum(jnp.where(sel, full, 0.0), axis=0, keepdims=True)


def _sb_decode_kernel(pt_ref, q_ref, u_ref, cache_ref, o_ref, buf_ref, sem_ref, acc_ref, carry_ref, *, n_pages):
    b = pl.program_id(0)

    def page_copy(p, slot, seq=b):
        return pltpu.make_async_copy(cache_ref.at[pt_ref[seq, p]], buf_ref.at[slot], sem_ref.at[slot])

    def start_latest(seq):
        page_copy(n_pages - 1, (n_pages - 1) % 2, seq).start()
        if n_pages > 1:
            page_copy(n_pages - 2, (n_pages - 2) % 2, seq).start()

    @pl.when(b == 0)
    def _():
        start_latest(b)

    acc_ref[...] = jnp.zeros_like(acc_ref)
    carry_ref[...] = jnp.zeros_like(carry_ref)
    qbd = _block_diag_rows(q_ref[...], WIDTH)
    u = u_ref[...]

    def page_step(p, slot):
        page_copy(p, slot).wait()
        kt = buf_ref[slot, 0].astype(BF16)
        vt = buf_ref[slot, 1].astype(BF16)
        z = _dot(qbd, kt)
        sp = _softplus(z)
        lk = -sp
        hi, lo = _split_bf16(lk)
        tail = _dot(hi, u) + _dot(lo, u)
        w = jnp.exp(z - sp + tail + carry_ref[...])
        acc_ref[...] += _dot_t(w.astype(BF16), vt)
        carry_ref[...] += jnp.sum(lk, axis=1, keepdims=True)
        return jnp.max(carry_ref[:N_HEADS, :])

    page_step(n_pages - 1, (n_pages - 1) % 2)
    mx2 = page_step(n_pages - 2, (n_pages - 2) % 2)
    deeper = (mx2 >= SB_EXIT) & (n_pages > 2)

    @pl.when(deeper)
    def _():
        page_copy(n_pages - 3, (n_pages - 3) % 2).start()
        if n_pages > 3:
            page_copy(n_pages - 4, (n_pages - 4) % 2).start()

    def cond(c):
        p, mx = c
        return (p >= 0) & (mx >= SB_EXIT)

    def body(c):
        p, _ = c
        slot = lax.rem(p, 2)
        mx = page_step(p, slot)

        @pl.when((mx >= SB_EXIT) & (p >= 2))
        def _():
            page_copy(p - 2, slot).start()

        return p - 1, mx

    p_end, _ = lax.while_loop(cond, body, (jnp.int32(n_pages - 3), mx2))

    @pl.when(deeper & (p_end >= 0))
    def _():
        page_copy(p_end, lax.rem(p_end, 2)).wait()

    @pl.when(b + 1 < pl.num_programs(0))
    def _():
        start_latest(b + 1)

    o_ref[...] = _diag_heads(acc_ref[...])


def _sb_decode(page_table, cache_t, qa_s, umat):
    db, n_pages = page_table.shape
    assert n_pages >= 4
    page_shape = cache_t.shape[1:]
    grid_spec = pltpu.PrefetchScalarGridSpec(
        num_scalar_prefetch=1,
        grid=(db,),
        in_specs=[pl.BlockSpec((None, 1, WIDTH), lambda b, pt: (b, 0, 0)),
                  pl.BlockSpec(umat.shape, lambda b, pt: (0, 0)),
                  pl.BlockSpec(memory_space=pl.ANY)],
        out_specs=pl.BlockSpec((None, 1, WIDTH), lambda b, pt: (b, 0, 0)),
        scratch_shapes=[pltpu.VMEM((2,) + page_shape, F32), pltpu.SemaphoreType.DMA((2,)),
                        pltpu.VMEM((DEC_ROWS, WIDTH), F32), pltpu.VMEM((DEC_ROWS, 1), F32)],
    )
    return pl.pallas_call(
        functools.partial(_sb_decode_kernel, n_pages=n_pages),
        grid_spec=grid_spec,
        out_shape=jax.ShapeDtypeStruct((db, 1, WIDTH), F32),
        compiler_params=_cparams(("arbitrary",), 16 << 20),
        name="sb_decode",
    )(page_table, qa_s.reshape(db, 1, WIDTH), umat, cache_t)


def _mla_decode_kernel(pt_ref, qn_ref, qr_ref, lat_ref, wuk_ref, wuv_ref, gsel_ref, cache_ref, o_ref,
                       buf_ref, sem_ref, *, chunk_pages):
    b = pl.program_id(0)
    n_chunks = buf_ref.shape[0]

    def page_copy(seq, c, k):
        return pltpu.make_async_copy(cache_ref.at[pt_ref[seq, c * chunk_pages + k]], buf_ref.at[c, k], sem_ref.at[c])

    def start_chunk(seq, c):
        for k in range(chunk_pages):
            page_copy(seq, c, k).start()

    @pl.when(b == 0)
    def _():
        for c in range(n_chunks):
            start_chunk(b, c)

    ql = _dot(_block_diag_rows(qn_ref[...], WIDTH), wuk_ref[...]).astype(BF16)
    shape = (DEC_ROWS, 2 * LANES)
    lane, rowi = _iota(shape, 1), _iota(shape, 0)
    sel = (((lane >= LANES) == (rowi >= 4)) & ((lax.shift_right_logical(lane, 4) & 3) == (rowi & 3))
           & (rowi < N_HEADS))
    qsel = jnp.where(sel, jnp.broadcast_to(qr_ref[...].astype(F32), shape), 0.0).astype(BF16)
    qrope = _dot(qsel, gsel_ref[...]).astype(BF16)
    own = lat_ref[...].astype(BF16).astype(F32)
    m = (jnp.sum(ql.astype(F32) * own[:, :B_KV_RANK], axis=1, keepdims=True)
         + jnp.sum(qrope.astype(F32) * own[:, B_KV_RANK:], axis=1, keepdims=True)) * MLA_SCALE
    l = jnp.ones_like(m)
    acc = jnp.broadcast_to(own[:, :B_KV_RANK], (DEC_ROWS, B_KV_RANK))

    for c in range(n_chunks):
        for k in range(chunk_pages):
            page_copy(b, c, k).wait()
        ckv = jnp.concatenate([buf_ref[c, k, :B_KV_RANK, :].astype(BF16) for k in range(chunk_pages)], axis=1)
        kro = jnp.concatenate([buf_ref[c, k, B_KV_RANK:, :].astype(BF16) for k in range(chunk_pages)], axis=1)
        sc = (_dot(ql, ckv) + _dot(qrope, kro)) * MLA_SCALE
        m_new = jnp.maximum(m, jnp.max(sc, axis=1, keepdims=True))
        pr = jnp.exp(sc - m_new)
        alpha = jnp.exp(m - m_new)
        l = alpha * l + jnp.sum(pr, axis=1, keepdims=True)
        acc = alpha * acc + _dot_t(pr.astype(BF16), ckv)
        m = m_new

        @pl.when(b + 1 < pl.num_programs(0))
        def _():
            start_chunk(b + 1, c)

    o_ref[...] = _diag_heads(_dot((acc / l).astype(BF16), wuv_ref[...]))


def _mla_decode(page_table, cache_t, qn_s, qr_s, lat_s, wuk_t, wuv, gsel, n_chunks):
    db, n_pages = page_table.shape
    assert n_pages % n_chunks == 0
    chunk_pages = n_pages // n_chunks
    per_b = lambda n: pl.BlockSpec((None, 1, n), lambda b, pt: (b, 0, 0))
    full = lambda a: pl.BlockSpec(a.shape, lambda b, pt: (0,) * a.ndim)
    grid_spec = pltpu.PrefetchScalarGridSpec(
        num_scalar_prefetch=1,
        grid=(db,),
        in_specs=[per_b(WIDTH), per_b(2 * LANES), per_b(B_LATENT), full(wuk_t), full(wuv), full(gsel),
                  pl.BlockSpec(memory_space=pl.ANY)],
        out_specs=per_b(WIDTH),
        scratch_shapes=[pltpu.VMEM((n_chunks, chunk_pages) + cache_t.shape[1:], F32),
                        pltpu.SemaphoreType.DMA((n_chunks,))],
    )
    return pl.pallas_call(
        functools.partial(_mla_decode_kernel, chunk_pages=chunk_pages),
        grid_spec=grid_spec,
        out_shape=jax.ShapeDtypeStruct((db, 1, WIDTH), F32),
        compiler_params=_cparams(("arbitrary",), 32 << 20),
        name="mla_decode",
    )(page_table, qn_s.reshape(db, 1, WIDTH), qr_s.reshape(db, 1, 2 * LANES), lat_s.reshape(db, 1, B_LATENT),
      wuk_t, wuv, gsel, cache_t)


def _dil_decode_kernel(q_ref, kvn_ref, c0_ref, c1_ref, c2_ref, o_ref):
    caches = (c0_ref, c1_ref, c2_ref)
    head = _iota((DEC_ROWS, 1), 0).astype(F32)
    outs, lses = [], []
    for g in range(C_GROUPS):
        r = C_DILATIONS[g]
        win = caches[g].shape[-1]
        slope = jnp.exp((-8.0 * (g * N_HEADS + head + 1.0) / (C_GROUPS * N_HEADS)) * np.log(2.0))
        qbd = _block_diag_rows(q_ref[:, g * WIDTH:(g + 1) * WIDTH], WIDTH)
        kt = caches[g][0].astype(BF16)
        vt = caches[g][1].astype(BF16)
        k_new = kvn_ref[:, g * 2 * WIDTH:g * 2 * WIDTH + WIDTH].astype(BF16)
        v_new = kvn_ref[:, g * 2 * WIDTH + WIDTH:(g + 1) * 2 * WIDTH].astype(BF16).astype(F32)
        row = _iota((DEC_ROWS, win), 1)
        s_win = _dot(qbd, kt) - slope * (win - row).astype(F32)
        s_win = jnp.where((row & (r - 1)) == 0, s_win, NEG_INF)
        s_own = jnp.sum(qbd.astype(F32) * k_new.astype(F32), axis=1, keepdims=True)
        mx = jnp.maximum(jnp.max(s_win, axis=1, keepdims=True), s_own)
        p_win = jnp.exp(s_win - mx)
        p_own = jnp.exp(s_own - mx)
        den = jnp.sum(p_win, axis=1, keepdims=True) + p_own
        o = (_dot_t(p_win.astype(BF16), vt) + p_own.astype(BF16).astype(F32) * v_new) / den
        outs.append(o)
        lses.append(mx + jnp.log(den))
    mx = functools.reduce(jnp.maximum, lses)
    es = [jnp.exp(l - mx) for l in lses]
    tot = functools.reduce(lambda a, c: a + c, es)
    merged = functools.reduce(lambda a, c: a + c, [(e / tot) * o for e, o in zip(es, outs)])
    o_ref[...] = _diag_heads(merged)


def _dil_decode(q3, kvn3, caches_t):
    db = q3.shape[0]
    per_b = lambda n: pl.BlockSpec((None, 1, n), lambda b: (b, 0, 0))
    specs = []
    for g, c in enumerate(caches_t):
        assert c.shape[-1] == C_STEPS * C_DILATIONS[g]
        specs.append(pl.BlockSpec((None,) + c.shape[1:], lambda b: (b, 0, 0, 0)))
    return pl.pallas_call(
        _dil_decode_kernel,
        grid=(db,),
        in_specs=[per_b(C_GROUPS * WIDTH), per_b(C_GROUPS * 2 * WIDTH)] + specs,
        out_specs=per_b(WIDTH),
        out_shape=jax.ShapeDtypeStruct((db, 1, WIDTH), F32),
        compiler_params=_cparams(("parallel",), 48 << 20),
        name="dil_decode",
    )(q3.reshape(db, 1, C_GROUPS * WIDTH), kvn3.reshape(db, 1, C_GROUPS * 2 * WIDTH), *caches_t)


def _prep_w_in0(w_in0):
    cuts = np.cumsum(IN0_SIZES)[:-1].tolist()
    qa, ka, va, dq, dkv, kr, gate = jnp.split(w_in0, cuts, axis=1)
    kr_sw = jnp.concatenate([kr[:, B_HALF:], kr[:, :B_HALF]], axis=1)
    wq = jnp.concatenate([qa, gate, dq], axis=1).astype(BF16)
    wk = jnp.concatenate([ka, va, dkv, kr, kr_sw], axis=1).T.astype(BF16)
    return wq, wk


def _prep_w_uq(w_uq):
    w = w_uq.reshape(B_Q_RANK, N_HEADS, B_NOPE + B_ROPE)
    nope = w[:, :, :B_NOPE].reshape(B_Q_RANK, WIDTH)
    x1 = w[:, :, B_NOPE:B_NOPE + B_HALF]
    x2 = w[:, :, B_NOPE + B_HALF:]
    slabs, slabs_sw = [], []
    for s in range(2):
        a = x1[:, 4 * s:4 * s + 4].reshape(B_Q_RANK, 4 * B_HALF)
        c = x2[:, 4 * s:4 * s + 4].reshape(B_Q_RANK, 4 * B_HALF)
        slabs.append(jnp.concatenate([a, c], axis=1))
        slabs_sw.append(jnp.concatenate([c, a], axis=1))
    return jnp.concatenate([nope] + slabs + slabs_sw, axis=1).astype(BF16)


def _rope_tables(pos):
    inv = ROPE_THETA ** (-jnp.arange(0, B_ROPE, 2, dtype=F32) / B_ROPE)
    ang = pos.astype(F32)[:, None] * inv[None, :]
    cos, sin = jnp.cos(ang), jnp.sin(ang)
    tabq = jnp.concatenate([jnp.tile(cos, (1, 8)), jnp.tile(-sin, (1, 4)), jnp.tile(sin, (1, 4))], axis=1)
    tabk = jnp.concatenate([cos, cos, -sin, sin], axis=1).T
    return tabq, tabk


def _const_mats():
    tri = lambda n: jnp.asarray((np.arange(n)[:, None] > np.arange(n)[None, :]).astype(np.float32), BF16)
    emat = np.zeros((B_ROPE, LANES), np.float32)
    for c in range(LANES):
        emat[(c % B_HALF) + (B_HALF if c >= 4 * B_HALF else 0), c] = 1.0
    gsel = np.zeros((2 * LANES, B_ROPE), np.float32)
    for c in range(2 * LANES):
        cc = c % LANES
        gsel[c, (cc % B_HALF) + (B_HALF if cc >= 4 * B_HALF else 0)] = 1.0
    return tri(MXU_TILE), tri(LANES), jnp.asarray(emat.T, BF16), jnp.asarray(gsel, BF16)


def _kv_rows(a_t, n, t):
    return jnp.transpose(a_t.reshape(n, 2, N_HEADS, HEAD_DIM, t), (0, 4, 1, 2, 3))


def _kv_pages(c):
    return jnp.transpose(c, (0, 2, 3, 4, 1)).reshape(c.shape[0], 2, WIDTH, c.shape[1])


def kernel(x_prompt, x_sample, cache_a_kv, cache_b_latent, cache_c0_kv, cache_c1_kv, cache_c2_kv,
           page_table, w_in0, g_cq, g_ckv, w_uq, w_uk, w_uv, w_out0, ln0_g, ln0_b,
           w_in1, w_out1, ln1_g, ln1_b):
    bsz, t_p, _ = x_prompt.shape
    db, t_s, _ = x_sample.shape
    assert t_s == 1 and t_p % MXU_TILE == 0 and cache_a_kv.shape[1] == LANES
    past_len = page_table.shape[1] * cache_a_kv.shape[1]
    m_p, m_s = bsz * t_p, db * t_s
    tm_p, tm_s = MXU_TILE, m_s

    tri_blk, tri_page, emat_t, gsel = _const_mats()
    wq0, wk0 = _prep_w_in0(w_in0)
    wuq = _prep_w_uq(w_uq)
    wuk_b, wuv_b = w_uk.astype(BF16), w_uv.astype(BF16)
    wo0, wo1, w1 = w_out0.astype(BF16), w_out1.astype(BF16), w_in1.astype(BF16)
    row = lambda v: v.reshape(1, -1)
    tabq_p, tabk_p = _rope_tables(jnp.arange(t_p))
    tabq_s, tabk_s = _rope_tables(jnp.full((m_s,), past_len, jnp.int32))

    xp = x_prompt.reshape(m_p, D_MODEL)
    xs3 = x_sample.reshape(1, m_s, D_MODEL)
    xs = xs3.reshape(m_s, D_MODEL)

    proj = lambda x3, tq_, tk_, tm: _l0_proj(x3, tq_, tk_, wq0, wk0, row(g_cq), g_ckv.reshape(-1, 1), wuq,
                                             wuk_b.T, wuv_b.T, emat_t, tm)
    qa_p, gate_p, qn_p, qr_p, kvf_p, lat_p, kvb_p, kcat_p, vb_p = proj(x_prompt, tabq_p, tabk_p, tm_p)
    qa_s, gate_s, qn_s, qr_s, kvf_s, lat_s, _, _, _ = proj(xs3, tabq_s, tabk_s, tm_s)

    oa_p = _sb_attn(qa_p, kvb_p, jnp.concatenate([tri_blk, tri_blk], axis=0)).reshape(m_p, WIDTH)
    ob_p = _mla_attn(qn_p, qr_p, kcat_p, vb_p).reshape(m_p, WIDTH)
    h_p = _finish0(xp, oa_p, ob_p, gate_p.reshape(m_p, 2 * WIDTH), wo0, row(ln0_g), row(ln0_b), tm_p)

    lat_s_rows = lat_s[0].T
    oa_s = _sb_decode(page_table, _kv_pages(cache_a_kv), qa_s[0], tri_page).reshape(m_s, WIDTH)
    ob_s = _mla_decode(page_table, jnp.transpose(cache_b_latent, (0, 2, 1)), qn_s[0], qr_s[0], lat_s_rows,
                       wuk_b.T, wuv_b, gsel, n_chunks=2).reshape(m_s, WIDTH)
    h_s = _finish0(xs, oa_s, ob_s, gate_s[0], wo0, row(ln0_g), row(ln0_b), tm_s)

    h_p3 = h_p.reshape(bsz, t_p, D_MODEL)
    outs_p = _l1_proj_streams(h_p3, w1, tm_p)
    q_p, kvb1_p, gate1_p = outs_p[0:3], outs_p[3:6], outs_p[6]
    os_p, lses_p, c_new_p = [], [], []
    for g, c in enumerate((cache_c0_kv, cache_c1_kv, cache_c2_kv)):
        o, lse = _dil_attn(q_p[g], kvb1_p[g], g)
        os_p.append(o)
        lses_p.append(lse)
        win = min(c.shape[1], t_p)
        w_kv_t = w1[:, g * 3 * WIDTH + WIDTH:(g + 1) * 3 * WIDTH].T
        c_new_p.append(_kv_rows(_proj_t(h_p3, w_kv_t, win, min(win, 2 * MXU_TILE)), bsz, win))
    y_p = _finish1_streams(h_p3, os_p, lses_p, gate1_p, wo1, row(ln1_g), row(ln1_b), tm_p)

    outs_s = _l1_proj(h_s, w1, tm_s, F32)
    q_s, kvf1_s, gate1_s = outs_s[0:3], outs_s[3:6], outs_s[6]
    oc_s = _dil_decode(jnp.concatenate(q_s, axis=1), jnp.concatenate(kvf1_s, axis=1),
                       [_kv_pages(c) for c in (cache_c0_kv, cache_c1_kv, cache_c2_kv)]).reshape(m_s, WIDTH)
    y_s = _finish1(h_s, [oc_s], [], gate1_s, wo1, row(ln1_g), row(ln1_b), tm_s)

    kv5 = lambda a: a.reshape(db, t_s, 2, N_HEADS, HEAD_DIM)
    return (y_p.reshape(bsz, t_p, D_MODEL), y_s.reshape(db, t_s, D_MODEL),
            _kv_rows(kvf_p, bsz, t_p), kv5(kvf_s[0].T),
            jnp.transpose(lat_p, (0, 2, 1)), lat_s_rows.reshape(db, t_s, B_LATENT),
            c_new_p[0], kv5(kvf1_s[0]), c_new_p[1], kv5(kvf1_s[1]), c_new_p[2], kv5(kvf1_s[2]))
```

```python
import functools

import numpy as np
import jax
import jax.numpy as jnp
from jax import lax
from jax.experimental import pallas as pl
from jax.experimental.pallas import tpu as pltpu

D_MODEL = 1024
DEPTH = 2
HEAD_DIM = 64
N_HEADS = 8
WIDTH = N_HEADS * HEAD_DIM
ATTN_SCALE = HEAD_DIM ** -0.5
B_NOPE = 64
B_ROPE = 32
B_HALF = B_ROPE // 2
B_Q_RANK = 384
B_KV_RANK = 256
B_LATENT = B_KV_RANK + B_ROPE
MLA_SCALE = (B_NOPE + B_ROPE) ** -0.5
ROPE_THETA = 10000.0
C_DILATIONS = (1, 4, 16)
C_GROUPS = 3
C_STEPS = 128
NORM_EPS = 1e-5
DN_ALPHA = (2 * DEPTH) ** 0.25
IN0_SIZES = (WIDTH, WIDTH, WIDTH, B_Q_RANK, B_KV_RANK, B_ROPE, 2 * WIDTH)

LANES = 128
PAIR = 2 * HEAD_DIM
N_PAIRS = N_HEADS // 2
V7X_VMEM_BYTES = 64 * 1024 * 1024
MXU_TILE = 256
DEC_ROWS = 16

NEG_INF = float("-inf")
SB_EXIT = -104.0
F32 = jnp.float32
BF16 = jnp.bfloat16


def _cparams(semantics, vmem_bytes):
    return pltpu.CompilerParams(dimension_semantics=semantics,
                                vmem_limit_bytes=min(int(vmem_bytes), V7X_VMEM_BYTES - (4 << 20)))


def _dot(a, b):
    return jnp.dot(a, b, preferred_element_type=F32)


def _dot_t(a, b):
    return lax.dot_general(a, b, (((1,), (1,)), ((), ())), preferred_element_type=F32)


def _iota(shape, dim):
    return lax.broadcasted_iota(jnp.int32, shape, dim)


def _softplus(z):
    return jnp.maximum(z, 0.0) + jnp.log(1.0 + jnp.exp(-jnp.abs(z)))


def _split_bf16(x):
    hi = x.astype(BF16)
    lo = (x - hi.astype(F32)).astype(BF16)
    return hi, lo


def _pair_half_mask(rows, head):
    lane = _iota((rows, PAIR), 1)
    return (lane >= HEAD_DIM) if head % 2 else (lane < HEAD_DIM)


def _keep(mask, x):
    return jnp.where(mask, x.astype(F32), 0.0).astype(x.dtype)


def _head_of_lane(shape, dim):
    return lax.shift_right_logical(_iota(shape, dim), 6)


def _twice(x):
    return jnp.concatenate([x, x], axis=1)


_Q_QA, _Q_GATE, _Q_DQ, _N_Q0 = 0, WIDTH, 3 * WIDTH, 3 * WIDTH + B_Q_RANK
_K_KV, _K_DKV, _K_KR, _N_K0 = 0, 2 * WIDTH, 2 * WIDTH + B_KV_RANK, 2 * WIDTH + B_KV_RANK + 2 * B_ROPE
_N_UQ = WIDTH + 4 * LANES


def _l0_proj_kernel(x_ref, tabq_ref, tabk_ref, wq_ref, wk_ref, gcq_ref, gckv_ref, wuq_ref, wuk_ref, wuv_ref,
                    e_ref, qa_ref, gate_ref, qn_ref, qr_ref, kvf_ref, lat_ref, kvb_ref, kcat_ref, vb_ref):
    xb = x_ref[...].astype(BF16)
    qa_ref[...] = (_dot(xb, wq_ref[:, _Q_QA:_Q_GATE]) * ATTN_SCALE).astype(BF16)
    gate_ref[...] = _dot(xb, wq_ref[:, _Q_GATE:_Q_DQ]).astype(gate_ref.dtype)
    dq = _dot(xb, wq_ref[:, _Q_DQ:_N_Q0])
    dqn = dq * lax.rsqrt(jnp.mean(dq * dq, axis=-1, keepdims=True) + NORM_EPS) * gcq_ref[...]
    qf = _dot(dqn.astype(BF16), wuq_ref[...])
    qn_ref[...] = qf[:, :WIDTH].astype(BF16)
    cosq = tabq_ref[:, :LANES]
    sinq = tabq_ref[:, LANES:]
    for s in range(2):
        nat = qf[:, WIDTH + s * LANES:WIDTH + (s + 1) * LANES]
        swp = qf[:, WIDTH + (2 + s) * LANES:WIDTH + (3 + s) * LANES]
        qr_ref[:, s * LANES:(s + 1) * LANES] = (nat * cosq + swp * sinq).astype(BF16)
    kv = _dot_t(wk_ref[_K_KV:_K_DKV, :], xb)
    kvf_ref[...] = kv
    kvb_ref[...] = kv.astype(BF16)
    dkv = _dot_t(wk_ref[_K_DKV:_K_KR, :], xb)
    ckv = dkv * lax.rsqrt(jnp.mean(dkv * dkv, axis=0, keepdims=True) + NORM_EPS) * gckv_ref[...]
    lat_ref[:B_KV_RANK, :] = ckv
    kr2 = _dot_t(wk_ref[_K_KR:_N_K0, :], xb)
    krot = kr2[:B_ROPE, :] * tabk_ref[:B_ROPE, :] + kr2[B_ROPE:, :] * tabk_ref[B_ROPE:, :]
    lat_ref[B_KV_RANK:, :] = krot
    ckv_b = ckv.astype(BF16)
    kn = _dot(wuk_ref[...], ckv_b).astype(BF16)
    vb_ref[...] = _dot(wuv_ref[...], ckv_b).astype(BF16)
    kslab = _dot(e_ref[...], krot.astype(BF16)).astype(BF16)
    for p in range(N_PAIRS):
        kcat_ref[p * MXU_TILE:p * MXU_TILE + PAIR, :] = kn[p * PAIR:(p + 1) * PAIR, :]
        kcat_ref[p * MXU_TILE + PAIR:(p + 1) * MXU_TILE, :] = kslab


def _l0_proj(x3, tabq, tabk, wq, wk, gcq, gckv, wuq, wuk_t, wuv_t, emat_t, tm):
    b, t, _ = x3.shape
    nt = t // tm
    tok = lambda n: pl.BlockSpec((None, tm, n), lambda bi, ti: (bi, ti, 0))
    feat = lambda n: pl.BlockSpec((None, n, tm), lambda bi, ti: (bi, 0, ti))
    featb = lambda n: pl.BlockSpec((None, None, n, tm), lambda bi, ti: (bi, ti, 0, 0))
    full = lambda a: pl.BlockSpec(a.shape, lambda bi, ti: (0,) * a.ndim, pipeline_mode=pl.Buffered(1))
    s = jax.ShapeDtypeStruct
    return pl.pallas_call(
        _l0_proj_kernel,
        grid=(b, nt),
        in_specs=[tok(D_MODEL), pl.BlockSpec((tm, 2 * LANES), lambda bi, ti: (ti, 0)),
                  pl.BlockSpec((2 * B_ROPE, tm), lambda bi, ti: (0, ti)),
                  full(wq), full(wk), full(gcq), full(gckv), full(wuq), full(wuk_t), full(wuv_t), full(emat_t)],
        out_specs=[tok(WIDTH), tok(2 * WIDTH), tok(WIDTH), tok(2 * LANES),
                   feat(2 * WIDTH), feat(B_LATENT),
                   featb(2 * WIDTH), featb(N_PAIRS * MXU_TILE), featb(WIDTH)],
        out_shape=[s((b, t, WIDTH), BF16), s((b, t, 2 * WIDTH), BF16), s((b, t, WIDTH), BF16),
                   s((b, t, 2 * LANES), BF16),
                   s((b, 2 * WIDTH, t), F32), s((b, B_LATENT, t), F32),
                   s((b, nt, 2 * WIDTH, tm), BF16), s((b, nt, N_PAIRS * MXU_TILE, tm), BF16),
                   s((b, nt, WIDTH, tm), BF16)],
        compiler_params=_cparams(("parallel", "parallel"), 48 << 20),
        name="l0_proj",
    )(x3, tabq, tabk, wq, wk, gcq, gckv, wuq, wuk_t, wuv_t, emat_t)


def _sb_attn_kernel(q_ref, kv_ref, u_ref, o_ref, qp_ref, acc_ref, carry_ref, *, tq):
    i = pl.program_id(1)
    acc_ref[...] = jnp.zeros_like(acc_ref)
    carry_ref[...] = jnp.zeros_like(carry_ref)
    for h in range(N_HEADS):
        qp_ref[h // 2, (h % 2) * tq:(h % 2 + 1) * tq, :] = _keep(
            _pair_half_mask(tq, h), q_ref[:, (h // 2) * PAIR:(h // 2 + 1) * PAIR])

    def block(j, masked):
        u = u_ref[...]
        if masked:
            strict = _iota((2 * tq, tq), 1) < (_iota((2 * tq, tq), 0) & (tq - 1))
        for p in range(N_PAIRS):
            kt = kv_ref[j, p * PAIR:(p + 1) * PAIR, :]
            vt = kv_ref[j, WIDTH + p * PAIR:WIDTH + (p + 1) * PAIR, :]
            z = _dot(qp_ref[p], kt)
            sp = _softplus(z)
            lk = jnp.where(strict, -sp, 0.0) if masked else -sp
            hi, lo = _split_bf16(lk)
            tail = _dot(jnp.concatenate([hi, lo], axis=1), u)
            w = jnp.exp(z - sp + tail + _twice(carry_ref[p]))
            if masked:
                w = jnp.where(strict, w, 0.0)
            acc_ref[p] += _dot_t(w.astype(BF16), vt)
            carry_ref[p] += jnp.sum(lk, axis=1, keepdims=True)

    def least_decayed():
        m = carry_ref[0]
        for p in range(1, N_PAIRS):
            m = jnp.maximum(m, carry_ref[p])
        return jnp.max(m)

    block(i, True)

    def cond(c):
        jj, mx = c
        return (jj <= i) & (mx >= SB_EXIT)

    def body(c):
        jj, _ = c
        block(i - jj, False)
        return jj + 1, least_decayed()

    lax.while_loop(cond, body, (jnp.int32(1), least_decayed()))
    for p in range(N_PAIRS):
        o_ref[:, p * PAIR:(p + 1) * PAIR] = jnp.where(
            _pair_half_mask(tq, 0), acc_ref[p, :tq, :], acc_ref[p, tq:, :]).astype(o_ref.dtype)


def _sb_attn(qa, kvb, umat):
    b, t, _ = qa.shape
    _, nblk, _, tq = kvb.shape
    return pl.pallas_call(
        functools.partial(_sb_attn_kernel, tq=tq),
        grid=(b, nblk),
        in_specs=[pl.BlockSpec((None, tq, WIDTH), lambda bi, i: (bi, i, 0)),
                  pl.BlockSpec((None, nblk, 2 * WIDTH, tq), lambda bi, i: (bi, 0, 0, 0)),
                  pl.BlockSpec(umat.shape, lambda bi, i: (0, 0))],
        out_specs=pl.BlockSpec((None, tq, WIDTH), lambda bi, i: (bi, i, 0)),
        out_shape=jax.ShapeDtypeStruct((b, t, WIDTH), BF16),
        scratch_shapes=[pltpu.VMEM((N_PAIRS, 2 * tq, PAIR), BF16), pltpu.VMEM((N_PAIRS, 2 * tq, PAIR), F32),
                        pltpu.VMEM((N_PAIRS, 2 * tq, PAIR), F32)],
        compiler_params=_cparams(("parallel", "arbitrary"), 40 << 20),
        name="sb_attn",
    )(qa, kvb, umat)


_MLA_EXP2 = MLA_SCALE * float(np.log2(np.e))


def _rope_slab_mask(rows, head):
    lane = _iota((rows, LANES), 1)
    return (lax.shift_right_logical(lane, 4) & 3) == (head % 4)


def _mla_attn_kernel(qn_ref, qr_ref, kcat_ref, vb_ref, o_ref, qc_ref, acc_ref, m_ref, l_ref, *, tq):
    i = pl.program_id(1)
    acc_ref[...] = jnp.zeros_like(acc_ref)
    m_ref[...] = jnp.full_like(m_ref, NEG_INF)
    l_ref[...] = jnp.zeros_like(l_ref)
    for h in range(N_HEADS):
        rows = slice((h % 2) * tq, (h % 2 + 1) * tq)
        qc_ref[h // 2, rows, :PAIR] = _keep(_pair_half_mask(tq, h), qn_ref[:, (h // 2) * PAIR:(h // 2 + 1) * PAIR])
        qc_ref[h // 2, rows, PAIR:] = _keep(_rope_slab_mask(tq, h), qr_ref[:, (h // 4) * LANES:(h // 4 + 1) * LANES])

    def block(j, masked):
        if masked:
            causal = _iota((2 * tq, tq), 1) <= (_iota((2 * tq, tq), 0) & (tq - 1))
        for p in range(N_PAIRS):
            kt = kcat_ref[j, p * MXU_TILE:(p + 1) * MXU_TILE, :]
            vt = vb_ref[j, p * PAIR:(p + 1) * PAIR, :]
            s = _dot(qc_ref[p], kt)
            if masked:
                s = jnp.where(causal, s, NEG_INF)
            m_old = m_ref[p]
            m_new = jnp.maximum(m_old, jnp.max(s, axis=1, keepdims=True))
            pr = jnp.exp2((s - _twice(m_new)) * _MLA_EXP2)
            alpha = jnp.exp2((m_old - m_new) * _MLA_EXP2)
            od = _dot_t(pr.astype(BF16), jnp.concatenate([vt, jnp.ones((PAIR, tq), BF16)], axis=0))
            l_ref[p] = alpha * l_ref[p] + od[:, PAIR:]
            acc_ref[p] = alpha * acc_ref[p] + od[:, :PAIR]
            m_ref[p] = m_new

    block(i, True)

    def body(j, c):
        block(j, False)
        return c

    lax.fori_loop(0, i, body, 0)
    for p in range(N_PAIRS):
        o_ref[:, p * PAIR:(p + 1) * PAIR] = jnp.where(
            _pair_half_mask(tq, 0), acc_ref[p, :tq, :] / l_ref[p, :tq, :],
            acc_ref[p, tq:, :] / l_ref[p, tq:, :]).astype(o_ref.dtype)


def _mla_attn(qn, qr, kcat, vb):
    b, t, _ = qn.shape
    _, nblk, _, tq = kcat.shape
    return pl.pallas_call(
        functools.partial(_mla_attn_kernel, tq=tq),
        grid=(b, nblk),
        in_specs=[pl.BlockSpec((None, tq, WIDTH), lambda bi, i: (bi, i, 0)),
                  pl.BlockSpec((None, tq, 2 * LANES), lambda bi, i: (bi, i, 0)),
                  pl.BlockSpec((None, nblk, N_PAIRS * MXU_TILE, tq), lambda bi, i: (bi, 0, 0, 0)),
                  pl.BlockSpec((None, nblk, WIDTH, tq), lambda bi, i: (bi, 0, 0, 0))],
        out_specs=pl.BlockSpec((None, tq, WIDTH), lambda bi, i: (bi, i, 0)),
        out_shape=jax.ShapeDtypeStruct((b, t, WIDTH), BF16),
        scratch_shapes=[pltpu.VMEM((N_PAIRS, 2 * tq, MXU_TILE), BF16), pltpu.VMEM((N_PAIRS, 2 * tq, PAIR), F32),
                        pltpu.VMEM((N_PAIRS, 2 * tq, PAIR), F32), pltpu.VMEM((N_PAIRS, 2 * tq, PAIR), F32)],
        compiler_params=_cparams(("parallel", "arbitrary"), 40 << 20),
        name="mla_attn",
    )(qn, qr, kcat, vb)


def _layer_norm(y, g, b):
    mu = jnp.mean(y, axis=-1, keepdims=True)
    d = y - mu
    var = jnp.mean(d * d, axis=-1, keepdims=True)
    return d * lax.rsqrt(var + NORM_EPS) * g + b


def _silu(g):
    return g / (1.0 + jnp.exp(-g))


def _finish0_kernel(x_ref, oa_ref, ob_ref, gate_ref, w_ref, g_ref, b_ref, h_ref):
    sg = _silu(gate_ref[...].astype(F32))
    y = (_dot((oa_ref[...].astype(F32) * sg[:, :WIDTH]).astype(BF16), w_ref[:WIDTH, :])
         + _dot((ob_ref[...].astype(F32) * sg[:, WIDTH:]).astype(BF16), w_ref[WIDTH:, :]))
    h_ref[...] = _layer_norm(DN_ALPHA * x_ref[...] + y, g_ref[...], b_ref[...])


def _finish0(x2, oa, ob, gate, w, g, b, tm):
    m = x2.shape[0]
    row = lambda n: pl.BlockSpec((tm, n), lambda i: (i, 0))
    full = lambda a: pl.BlockSpec(a.shape, lambda i: (0,) * a.ndim)
    return pl.pallas_call(
        _finish0_kernel,
        grid=(m // tm,),
        in_specs=[row(D_MODEL), row(WIDTH), row(WIDTH), row(2 * WIDTH), full(w), full(g), full(b)],
        out_specs=row(D_MODEL),
        out_shape=jax.ShapeDtypeStruct((m, D_MODEL), F32),
        compiler_params=_cparams(("parallel",), 40 << 20),
        name="finish0",
    )(x2, oa, ob, gate, w, g, b)


def _finish1_kernel(*refs, n_groups):
    x_ref = refs[0]
    o_refs = refs[1:1 + n_groups]
    l_refs = refs[1 + n_groups:1 + 2 * n_groups] if n_groups > 1 else ()
    gate_ref, w_ref, g_ref, b_ref, h_ref = refs[-5:]
    if n_groups > 1:
        lses = [r[...] for r in l_refs]
        mx = functools.reduce(jnp.maximum, lses)
        es = [jnp.exp(l - mx) for l in lses]
        num = functools.reduce(lambda a, c: a + c, [e * r[...] for e, r in zip(es, o_refs)])
        o = num / functools.reduce(lambda a, c: a + c, es)
    else:
        o = o_refs[0][...]
    y = _dot((o * _silu(gate_ref[...].astype(F32))).astype(BF16), w_ref[...])
    h_ref[...] = _layer_norm(DN_ALPHA * x_ref[...] + y, g_ref[...], b_ref[...])


def _finish1_streams_kernel(x_ref, *refs, tm):
    o_refs, l_refs = refs[0:C_GROUPS], refs[C_GROUPS:2 * C_GROUPS]
    gate_ref, w_ref, g_ref, b_ref, h_ref, tok_ref = refs[2 * C_GROUPS:]
    n_slabs = WIDTH // LANES

    def to_tokens(ref, r, slot):
        if r == 1:
            return ref[0].astype(F32)
        for rho in range(r):
            for c in range(n_slabs):
                tok_ref[slot, c, pl.ds(rho, tm // r, stride=r), :] = (
                    ref[rho, :, c * LANES:(c + 1) * LANES].astype(F32))
        return jnp.concatenate([tok_ref[slot, c] for c in range(n_slabs)], axis=1)

    lses = [to_tokens(l_refs[g], r, 2 * g) for g, r in enumerate(C_DILATIONS)]
    mx = functools.reduce(jnp.maximum, lses)
    es = [jnp.exp(l - mx) for l in lses]
    den = functools.reduce(lambda a, c: a + c, es)
    num = functools.reduce(lambda a, c: a + c,
                           [e * to_tokens(o_refs[g], r, 2 * g + 1) for g, (e, r) in enumerate(zip(es, C_DILATIONS))])
    y = _dot((num / den * _silu(gate_ref[...].astype(F32))).astype(BF16), w_ref[...])
    h_ref[...] = _layer_norm(DN_ALPHA * x_ref[...] + y, g_ref[...], b_ref[...])


def _finish1_streams(x3, os_, lses, gate3, w, g, b, tm):
    bsz, t, _ = x3.shape
    tok = lambda n: pl.BlockSpec((None, tm, n), lambda bi, ti: (bi, ti, 0))
    stream = lambda r: pl.BlockSpec((None, r, tm // r, WIDTH), lambda bi, ti: (bi, 0, ti, 0))
    full = lambda a: pl.BlockSpec(a.shape, lambda bi, ti: (0,) * a.ndim)
    return pl.pallas_call(
        functools.partial(_finish1_streams_kernel, tm=tm),
        grid=(bsz, t // tm),
        in_specs=([tok(D_MODEL)] + [stream(r) for r in C_DILATIONS] * 2 + [tok(WIDTH), full(w), full(g), full(b)]),
        out_specs=tok(D_MODEL),
        out_shape=jax.ShapeDtypeStruct((bsz, t, D_MODEL), F32),
        scratch_shapes=[pltpu.VMEM((2 * C_GROUPS, WIDTH // LANES, tm, LANES), F32)],
        compiler_params=_cparams(("parallel", "parallel"), 40 << 20),
        name="finish1_streams",
    )(x3, *os_, *lses, gate3, w, g, b)


def _finish1(x2, os_, lses, gate, w, g, b, tm):
    m = x2.shape[0]
    n_groups = len(os_)
    row = lambda n: pl.BlockSpec((tm, n), lambda i: (i, 0))
    full = lambda a: pl.BlockSpec(a.shape, lambda i: (0,) * a.ndim)
    args = [x2, *os_, *lses, gate, w, g, b]
    in_specs = ([row(D_MODEL)] + [row(WIDTH)] * (len(os_) + len(lses)) + [row(WIDTH), full(w), full(g), full(b)])
    return pl.pallas_call(
        functools.partial(_finish1_kernel, n_groups=n_groups),
        grid=(m // tm,),
        in_specs=in_specs,
        out_specs=row(D_MODEL),
        out_shape=jax.ShapeDtypeStruct((m, D_MODEL), F32),
        compiler_params=_cparams(("parallel",), 40 << 20),
        name="finish1",
    )(*args)


def _l1_proj_kernel(x_ref, w_ref, *out_refs, kv_dtype):
    xb = x_ref[...].astype(BF16)
    q_refs = out_refs[0:C_GROUPS]
    kv_refs = out_refs[C_GROUPS:2 * C_GROUPS]
    gate_ref = out_refs[2 * C_GROUPS]
    for g in range(C_GROUPS):
        base = g * 3 * WIDTH
        q_refs[g][...] = (_dot(xb, w_ref[:, base:base + WIDTH]) * ATTN_SCALE).astype(BF16)
        kv_refs[g][...] = _dot(xb, w_ref[:, base + WIDTH:base + 3 * WIDTH]).astype(kv_dtype)
    gate_ref[...] = _dot(xb, w_ref[:, C_GROUPS * 3 * WIDTH:])


def _l1_proj(x2, w1, tm, kv_dtype):
    m = x2.shape[0]
    row = lambda n: pl.BlockSpec((tm, n), lambda i: (i, 0))
    outs = [(WIDTH, BF16)] * C_GROUPS + [(2 * WIDTH, kv_dtype)] * C_GROUPS + [(WIDTH, F32)]
    return pl.pallas_call(
        functools.partial(_l1_proj_kernel, kv_dtype=kv_dtype),
        grid=(m // tm,),
        in_specs=[row(D_MODEL),
                  pl.BlockSpec(w1.shape, lambda i: (0, 0), pipeline_mode=pl.Buffered(1))],
        out_specs=[row(n) for n, _ in outs],
        out_shape=[jax.ShapeDtypeStruct((m, n), dt) for n, dt in outs],
        compiler_params=_cparams(("parallel",), 48 << 20),
        name="l1_proj",
    )(x2, w1)


def _stream_rows(slab_ref, r, tm):
    n_slabs = slab_ref.shape[0]
    return jnp.concatenate(
        [jnp.concatenate([slab_ref[c, pl.ds(rho, tm // r, stride=r), :] for c in range(n_slabs)], axis=1)
         for rho in range(r)], axis=0)


def _l1_proj_streams_kernel(x_ref, w_ref, *refs, tm):
    q_refs, kv_refs = refs[0:C_GROUPS], refs[C_GROUPS:2 * C_GROUPS]
    gate_ref, xs_ref = refs[2 * C_GROUPS], refs[2 * C_GROUPS + 1]
    for c in range(D_MODEL // LANES):
        xs_ref[c] = x_ref[:, c * LANES:(c + 1) * LANES]
    xb = x_ref[...].astype(BF16)
    gate_ref[...] = _dot(xb, w_ref[:, C_GROUPS * 3 * WIDTH:]).astype(gate_ref.dtype)
    for g in range(C_GROUPS):
        r = C_DILATIONS[g]
        rows = tm // r
        xg = xb if r == 1 else _stream_rows(xs_ref, r, tm).astype(BF16)
        base = g * 3 * WIDTH
        q = (_dot(xg, w_ref[:, base:base + WIDTH]) * ATTN_SCALE).astype(BF16)
        kv = _dot(xg, w_ref[:, base + WIDTH:base + 3 * WIDTH]).astype(BF16)
        for rho in range(r):
            q_refs[g][rho] = q[rho * rows:(rho + 1) * rows]
            kv_refs[g][rho] = kv[rho * rows:(rho + 1) * rows]


def _l1_proj_streams(x3, w1, tm):
    b, t, _ = x3.shape
    s = jax.ShapeDtypeStruct
    stream = lambda r, n: pl.BlockSpec((None, r, tm // r, n), lambda bi, ti: (bi, 0, ti, 0))
    out_specs = ([stream(r, WIDTH) for r in C_DILATIONS] + [stream(r, 2 * WIDTH) for r in C_DILATIONS]
                 + [pl.BlockSpec((None, tm, WIDTH), lambda bi, ti: (bi, ti, 0))])
    out_shape = ([s((b, r, t // r, WIDTH), BF16) for r in C_DILATIONS]
                 + [s((b, r, t // r, 2 * WIDTH), BF16) for r in C_DILATIONS] + [s((b, t, WIDTH), BF16)])
    return pl.pallas_call(
        functools.partial(_l1_proj_streams_kernel, tm=tm),
        grid=(b, t // tm),
        in_specs=[pl.BlockSpec((None, tm, D_MODEL), lambda bi, ti: (bi, ti, 0)),
                  pl.BlockSpec(w1.shape, lambda bi, ti: (0, 0), pipeline_mode=pl.Buffered(1))],
        out_specs=out_specs,
        out_shape=out_shape,
        scratch_shapes=[pltpu.VMEM((D_MODEL // LANES, tm, LANES), F32)],
        compiler_params=_cparams(("parallel", "parallel"), 48 << 20),
        name="l1_proj_streams",
    )(x3, w1)


def _proj_t_kernel(x_ref, w_ref, o_ref):
    o_ref[...] = _dot_t(w_ref[...], x_ref[...].astype(BF16))


def _proj_t(x3, w_t, window, tm):
    b, t, k = x3.shape
    f = w_t.shape[0]
    first = (t - window) // tm
    return pl.pallas_call(
        _proj_t_kernel,
        grid=(b, window // tm),
        in_specs=[pl.BlockSpec((None, tm, k), lambda bi, ti: (bi, first + ti, 0)),
                  pl.BlockSpec(w_t.shape, lambda bi, ti: (0, 0))],
        out_specs=pl.BlockSpec((None, f, tm), lambda bi, ti: (bi, 0, ti)),
        out_shape=jax.ShapeDtypeStruct((b, f, window), F32),
        compiler_params=_cparams(("parallel", "parallel"), 32 << 20),
        name="proj_t",
    )(x3, w_t)


def _alibi_slope(group, head):
    n = C_GROUPS * N_HEADS
    return float(2.0 ** (-8.0 * (group * N_HEADS + head + 1) / n))


def _dil_attn_kernel(q_ref, kv_ref, o_ref, lse_ref, *, group, n_blocks):
    r = C_DILATIONS[group]
    blk = C_STEPS

    def band(kn):
        qq = _iota((2 * blk, kn), 0) & (blk - 1)
        steps = qq - _iota((2 * blk, kn), 1) + (kn - blk)
        return (steps >= 0) & (steps <= C_STEPS), (r * steps).astype(F32)

    def block(si, n, first):
        qs = pl.multiple_of(n * blk, blk)
        ks = 0 if first else pl.multiple_of((n - 1) * blk, blk)
        kn = blk if first else 2 * blk
        valid, stepsf = band(kn)
        odd_row = _iota((2 * blk, 1), 0) >= blk
        even = _pair_half_mask(blk, 0)
        for p in range(N_PAIRS):
            qpair = q_ref[si, pl.ds(qs, blk), p * PAIR:(p + 1) * PAIR]
            kb = kv_ref[si, pl.ds(ks, kn), p * PAIR:(p + 1) * PAIR]
            vb = kv_ref[si, pl.ds(ks, kn), WIDTH + p * PAIR:WIDTH + (p + 1) * PAIR]
            q2 = jnp.concatenate([_keep(_pair_half_mask(blk, 0), qpair), _keep(_pair_half_mask(blk, 1), qpair)], axis=0)
            slope = jnp.where(odd_row, _alibi_slope(group, 2 * p + 1), _alibi_slope(group, 2 * p))
            s = jnp.where(valid, _dot_t(q2, kb) - slope * stepsf, NEG_INF)
            mx = jnp.max(s, axis=1, keepdims=True)
            pr = jnp.exp(s - mx)
            if first:
                den = jnp.sum(pr, axis=1, keepdims=True)
                o2 = _dot(pr.astype(BF16), vb) / den
                lse2 = jnp.broadcast_to(mx + jnp.log(den), (2 * blk, PAIR))
            else:
                od = _dot(pr.astype(BF16), jnp.concatenate([vb, jnp.ones((kn, PAIR), BF16)], axis=1))
                den = od[:, PAIR:]
                o2 = od[:, :PAIR] / den
                lse2 = mx + jnp.log(den)
            o_ref[si, pl.ds(qs, blk), p * PAIR:(p + 1) * PAIR] = jnp.where(even, o2[:blk], o2[blk:]).astype(BF16)
            lse_ref[si, pl.ds(qs, blk), p * PAIR:(p + 1) * PAIR] = jnp.where(even, lse2[:blk], lse2[blk:])

    def stream(si, c):
        block(si, 0, True)
        if n_blocks > 1:
            def body(n, cc):
                block(si, n, False)
                return cc
            lax.fori_loop(1, n_blocks, body, 0)
        return c

    if r == 1:
        stream(0, 0)
    else:
        lax.fori_loop(0, r, stream, 0)


def _dil_attn(q, kvb, group):
    b, r, ls, _ = q.shape
    spec = lambda n: pl.BlockSpec((None, r, ls, n), lambda bi: (bi, 0, 0, 0))
    return pl.pallas_call(
        functools.partial(_dil_attn_kernel, group=group, n_blocks=ls // C_STEPS),
        grid=(b,),
        in_specs=[spec(WIDTH), spec(2 * WIDTH)],
        out_specs=[spec(WIDTH)] * 2,
        out_shape=[jax.ShapeDtypeStruct((b, r, ls, WIDTH), BF16), jax.ShapeDtypeStruct((b, r, ls, WIDTH), F32)],
        compiler_params=_cparams(("parallel",), 48 << 20),
        name=f"dil_attn{group}",
    )(q, kvb)


def _block_diag_rows(row, width):
    sel = _head_of_lane((DEC_ROWS, width), 1) == _iota((DEC_ROWS, width), 0)
    return jnp.where(sel, jnp.broadcast_to(row.astype(F32), (DEC_ROWS, width)), 0.0).astype(row.dtype)


def _diag_heads(full):
    sel = _head_of_lane(full.shape, 1) == _iota(full.shape, 0)
    return jnp.sum(jnp.where(sel, full, 0.0), axis=0, keepdims=True)


def _sb_decode_kernel(pt_ref, q_ref, u_ref, cache_ref, o_ref, buf_ref, sem_ref, acc_ref, carry_ref, *, n_pages):
    b = pl.program_id(0)

    def page_copy(p, slot, seq=b):
        return pltpu.make_async_copy(cache_ref.at[pt_ref[seq, p]], buf_ref.at[slot], sem_ref.at[slot])

    def start_latest(seq):
        page_copy(n_pages - 1, (n_pages - 1) % 2, seq).start()
        if n_pages > 1:
            page_copy(n_pages - 2, (n_pages - 2) % 2, seq).start()

    @pl.when(b == 0)
    def _():
        start_latest(b)

    acc_ref[...] = jnp.zeros_like(acc_ref)
    carry_ref[...] = jnp.zeros_like(carry_ref)
    qbd = _block_diag_rows(q_ref[...], WIDTH)
    u = u_ref[...]

    def page_step(p, slot):
        page_copy(p, slot).wait()
        kt = buf_ref[slot, 0].astype(BF16)
        vt = buf_ref[slot, 1].astype(BF16)
        z = _dot(qbd, kt)
        sp = _softplus(z)
        lk = -sp
        hi, lo = _split_bf16(lk)
        tail = _dot(hi, u) + _dot(lo, u)
        w = jnp.exp(z - sp + tail + carry_ref[...])
        acc_ref[...] += _dot_t(w.astype(BF16), vt)
        carry_ref[...] += jnp.sum(lk, axis=1, keepdims=True)
        return jnp.max(carry_ref[:N_HEADS, :])

    page_step(n_pages - 1, (n_pages - 1) % 2)
    mx2 = page_step(n_pages - 2, (n_pages - 2) % 2)
    deeper = (mx2 >= SB_EXIT) & (n_pages > 2)

    @pl.when(deeper)
    def _():
        page_copy(n_pages - 3, (n_pages - 3) % 2).start()
        if n_pages > 3:
            page_copy(n_pages - 4, (n_pages - 4) % 2).start()

    def cond(c):
        p, mx = c
        return (p >= 0) & (mx >= SB_EXIT)

    def body(c):
        p, _ = c
        slot = lax.rem(p, 2)
        mx = page_step(p, slot)

        @pl.when((mx >= SB_EXIT) & (p >= 2))
        def _():
            page_copy(p - 2, slot).start()

        return p - 1, mx

    p_end, _ = lax.while_loop(cond, body, (jnp.int32(n_pages - 3), mx2))

    @pl.when(deeper & (p_end >= 0))
    def _():
        page_copy(p_end, lax.rem(p_end, 2)).wait()

    @pl.when(b + 1 < pl.num_programs(0))
    def _():
        start_latest(b + 1)

    o_ref[...] = _diag_heads(acc_ref[...])


def _sb_decode(page_table, cache_t, qa_s, umat):
    db, n_pages = page_table.shape
    assert n_pages >= 4
    page_shape = cache_t.shape[1:]
    grid_spec = pltpu.PrefetchScalarGridSpec(
        num_scalar_prefetch=1,
        grid=(db,),
        in_specs=[pl.BlockSpec((None, 1, WIDTH), lambda b, pt: (b, 0, 0)),
                  pl.BlockSpec(umat.shape, lambda b, pt: (0, 0)),
                  pl.BlockSpec(memory_space=pl.ANY)],
        out_specs=pl.BlockSpec((None, 1, WIDTH), lambda b, pt: (b, 0, 0)),
        scratch_shapes=[pltpu.VMEM((2,) + page_shape, F32), pltpu.SemaphoreType.DMA((2,)),
                        pltpu.VMEM((DEC_ROWS, WIDTH), F32), pltpu.VMEM((DEC_ROWS, 1), F32)],
    )
    return pl.pallas_call(
        functools.partial(_sb_decode_kernel, n_pages=n_pages),
        grid_spec=grid_spec,
        out_shape=jax.ShapeDtypeStruct((db, 1, WIDTH), F32),
        compiler_params=_cparams(("arbitrary",), 16 << 20),
        name="sb_decode",
    )(page_table, qa_s.reshape(db, 1, WIDTH), umat, cache_t)


def _mla_decode_kernel(pt_ref, qn_ref, qr_ref, lat_ref, wuk_ref, wuv_ref, gsel_ref, cache_ref, o_ref,
                       buf_ref, sem_ref, *, chunk_pages):
    b = pl.program_id(0)
    n_chunks = buf_ref.shape[0]

    def page_copy(seq, c, k):
        return pltpu.make_async_copy(cache_ref.at[pt_ref[seq, c * chunk_pages + k]], buf_ref.at[c, k], sem_ref.at[c])

    def start_chunk(seq, c):
        for k in range(chunk_pages):
            page_copy(seq, c, k).start()

    @pl.when(b == 0)
    def _():
        for c in range(n_chunks):
            start_chunk(b, c)

    ql = _dot(_block_diag_rows(qn_ref[...], WIDTH), wuk_ref[...]).astype(BF16)
    shape = (DEC_ROWS, 2 * LANES)
    lane, rowi = _iota(shape, 1), _iota(shape, 0)
    sel = (((lane >= LANES) == (rowi >= 4)) & ((lax.shift_right_logical(lane, 4) & 3) == (rowi & 3))
           & (rowi < N_HEADS))
    qsel = jnp.where(sel, jnp.broadcast_to(qr_ref[...].astype(F32), shape), 0.0).astype(BF16)
    qrope = _dot(qsel, gsel_ref[...]).astype(BF16)
    own = lat_ref[...].astype(BF16).astype(F32)
    m = (jnp.sum(ql.astype(F32) * own[:, :B_KV_RANK], axis=1, keepdims=True)
         + jnp.sum(qrope.astype(F32) * own[:, B_KV_RANK:], axis=1, keepdims=True)) * MLA_SCALE
    l = jnp.ones_like(m)
    acc = jnp.broadcast_to(own[:, :B_KV_RANK], (DEC_ROWS, B_KV_RANK))

    for c in range(n_chunks):
        for k in range(chunk_pages):
            page_copy(b, c, k).wait()
        ckv = jnp.concatenate([buf_ref[c, k, :B_KV_RANK, :].astype(BF16) for k in range(chunk_pages)], axis=1)
        kro = jnp.concatenate([buf_ref[c, k, B_KV_RANK:, :].astype(BF16) for k in range(chunk_pages)], axis=1)
        sc = (_dot(ql, ckv) + _dot(qrope, kro)) * MLA_SCALE
        m_new = jnp.maximum(m, jnp.max(sc, axis=1, keepdims=True))
        pr = jnp.exp(sc - m_new)
        alpha = jnp.exp(m - m_new)
        l = alpha * l + jnp.sum(pr, axis=1, keepdims=True)
        acc = alpha * acc + _dot_t(pr.astype(BF16), ckv)
        m = m_new

        @pl.when(b + 1 < pl.num_programs(0))
        def _():
            start_chunk(b + 1, c)

    o_ref[...] = _diag_heads(_dot((acc / l).astype(BF16), wuv_ref[...]))


def _mla_decode(page_table, cache_t, qn_s, qr_s, lat_s, wuk_t, wuv, gsel, n_chunks):
    db, n_pages = page_table.shape
    assert n_pages % n_chunks == 0
    chunk_pages = n_pages // n_chunks
    per_b = lambda n: pl.BlockSpec((None, 1, n), lambda b, pt: (b, 0, 0))
    full = lambda a: pl.BlockSpec(a.shape, lambda b, pt: (0,) * a.ndim)
    grid_spec = pltpu.PrefetchScalarGridSpec(
        num_scalar_prefetch=1,
        grid=(db,),
        in_specs=[per_b(WIDTH), per_b(2 * LANES), per_b(B_LATENT), full(wuk_t), full(wuv), full(gsel),
                  pl.BlockSpec(memory_space=pl.ANY)],
        out_specs=per_b(WIDTH),
        scratch_shapes=[pltpu.VMEM((n_chunks, chunk_pages) + cache_t.shape[1:], F32),
                        pltpu.SemaphoreType.DMA((n_chunks,))],
    )
    return pl.pallas_call(
        functools.partial(_mla_decode_kernel, chunk_pages=chunk_pages),
        grid_spec=grid_spec,
        out_shape=jax.ShapeDtypeStruct((db, 1, WIDTH), F32),
        compiler_params=_cparams(("arbitrary",), 32 << 20),
        name="mla_decode",
    )(page_table, qn_s.reshape(db, 1, WIDTH), qr_s.reshape(db, 1, 2 * LANES), lat_s.reshape(db, 1, B_LATENT),
      wuk_t, wuv, gsel, cache_t)


def _dil_decode_kernel(q_ref, kvn_ref, c0_ref, c1_ref, c2_ref, o_ref):
    caches = (c0_ref, c1_ref, c2_ref)
    head = _iota((DEC_ROWS, 1), 0).astype(F32)
    outs, lses = [], []
    for g in range(C_GROUPS):
        r = C_DILATIONS[g]
        win = caches[g].shape[-1]
        slope = jnp.exp((-8.0 * (g * N_HEADS + head + 1.0) / (C_GROUPS * N_HEADS)) * np.log(2.0))
        qbd = _block_diag_rows(q_ref[:, g * WIDTH:(g + 1) * WIDTH], WIDTH)
        kt = caches[g][0].astype(BF16)
        vt = caches[g][1].astype(BF16)
        k_new = kvn_ref[:, g * 2 * WIDTH:g * 2 * WIDTH + WIDTH].astype(BF16)
        v_new = kvn_ref[:, g * 2 * WIDTH + WIDTH:(g + 1) * 2 * WIDTH].astype(BF16).astype(F32)
        row = _iota((DEC_ROWS, win), 1)
        s_win = _dot(qbd, kt) - slope * (win - row).astype(F32)
        s_win = jnp.where((row & (r - 1)) == 0, s_win, NEG_INF)
        s_own = jnp.sum(qbd.astype(F32) * k_new.astype(F32), axis=1, keepdims=True)
        mx = jnp.maximum(jnp.max(s_win, axis=1, keepdims=True), s_own)
        p_win = jnp.exp(s_win - mx)
        p_own = jnp.exp(s_own - mx)
        den = jnp.sum(p_win, axis=1, keepdims=True) + p_own
        o = (_dot_t(p_win.astype(BF16), vt) + p_own.astype(BF16).astype(F32) * v_new) / den
        outs.append(o)
        lses.append(mx + jnp.log(den))
    mx = functools.reduce(jnp.maximum, lses)
    es = [jnp.exp(l - mx) for l in lses]
    tot = functools.reduce(lambda a, c: a + c, es)
    merged = functools.reduce(lambda a, c: a + c, [(e / tot) * o for e, o in zip(es, outs)])
    o_ref[...] = _diag_heads(merged)


def _dil_decode(q3, kvn3, caches_t):
    db = q3.shape[0]
    per_b = lambda n: pl.BlockSpec((None, 1, n), lambda b: (b, 0, 0))
    specs = []
    for g, c in enumerate(caches_t):
        assert c.shape[-1] == C_STEPS * C_DILATIONS[g]
        specs.append(pl.BlockSpec((None,) + c.shape[1:], lambda b: (b, 0, 0, 0)))
    return pl.pallas_call(
        _dil_decode_kernel,
        grid=(db,),
        in_specs=[per_b(C_GROUPS * WIDTH), per_b(C_GROUPS * 2 * WIDTH)] + specs,
        out_specs=per_b(WIDTH),
        out_shape=jax.ShapeDtypeStruct((db, 1, WIDTH), F32),
        compiler_params=_cparams(("parallel",), 48 << 20),
        name="dil_decode",
    )(q3.reshape(db, 1, C_GROUPS * WIDTH), kvn3.reshape(db, 1, C_GROUPS * 2 * WIDTH), *caches_t)


def _prep_w_in0(w_in0):
    cuts = np.cumsum(IN0_SIZES)[:-1].tolist()
    qa, ka, va, dq, dkv, kr, gate = jnp.split(w_in0, cuts, axis=1)
    kr_sw = jnp.concatenate([kr[:, B_HALF:], kr[:, :B_HALF]], axis=1)
    wq = jnp.concatenate([qa, gate, dq], axis=1).astype(BF16)
    wk = jnp.concatenate([ka, va, dkv, kr, kr_sw], axis=1).T.astype(BF16)
    return wq, wk


def _prep_w_uq(w_uq):
    w = w_uq.reshape(B_Q_RANK, N_HEADS, B_NOPE + B_ROPE)
    nope = w[:, :, :B_NOPE].reshape(B_Q_RANK, WIDTH)
    x1 = w[:, :, B_NOPE:B_NOPE + B_HALF]
    x2 = w[:, :, B_NOPE + B_HALF:]
    slabs, slabs_sw = [], []
    for s in range(2):
        a = x1[:, 4 * s:4 * s + 4].reshape(B_Q_RANK, 4 * B_HALF)
        c = x2[:, 4 * s:4 * s + 4].reshape(B_Q_RANK, 4 * B_HALF)
        slabs.append(jnp.concatenate([a, c], axis=1))
        slabs_sw.append(jnp.concatenate([c, a], axis=1))
    return jnp.concatenate([nope] + slabs + slabs_sw, axis=1).astype(BF16)


def _rope_tables(pos):
    inv = ROPE_THETA ** (-jnp.arange(0, B_ROPE, 2, dtype=F32) / B_ROPE)
    ang = pos.astype(F32)[:, None] * inv[None, :]
    cos, sin = jnp.cos(ang), jnp.sin(ang)
    tabq = jnp.concatenate([jnp.tile(cos, (1, 8)), jnp.tile(-sin, (1, 4)), jnp.tile(sin, (1, 4))], axis=1)
    tabk = jnp.concatenate([cos, cos, -sin, sin], axis=1).T
    return tabq, tabk


def _const_mats():
    tri = lambda n: jnp.asarray((np.arange(n)[:, None] > np.arange(n)[None, :]).astype(np.float32), BF16)
    emat = np.zeros((B_ROPE, LANES), np.float32)
    for c in range(LANES):
        emat[(c % B_HALF) + (B_HALF if c >= 4 * B_HALF else 0), c] = 1.0
    gsel = np.zeros((2 * LANES, B_ROPE), np.float32)
    for c in range(2 * LANES):
        cc = c % LANES
        gsel[c, (cc % B_HALF) + (B_HALF if cc >= 4 * B_HALF else 0)] = 1.0
    return tri(MXU_TILE), tri(LANES), jnp.asarray(emat.T, BF16), jnp.asarray(gsel, BF16)


def _kv_rows(a_t, n, t):
    return jnp.transpose(a_t.reshape(n, 2, N_HEADS, HEAD_DIM, t), (0, 4, 1, 2, 3))


def _kv_pages(c):
    return jnp.transpose(c, (0, 2, 3, 4, 1)).reshape(c.shape[0], 2, WIDTH, c.shape[1])


def kernel(x_prompt, x_sample, cache_a_kv, cache_b_latent, cache_c0_kv, cache_c1_kv, cache_c2_kv,
           page_table, w_in0, g_cq, g_ckv, w_uq, w_uk, w_uv, w_out0, ln0_g, ln0_b,
           w_in1, w_out1, ln1_g, ln1_b):
    bsz, t_p, _ = x_prompt.shape
    db, t_s, _ = x_sample.shape
    assert t_s == 1 and t_p % MXU_TILE == 0 and cache_a_kv.shape[1] == LANES
    past_len = page_table.shape[1] * cache_a_kv.shape[1]
    m_p, m_s = bsz * t_p, db * t_s
    tm_p, tm_s = MXU_TILE, m_s

    tri_blk, tri_page, emat_t, gsel = _const_mats()
    wq0, wk0 = _prep_w_in0(w_in0)
    wuq = _prep_w_uq(w_uq)
    wuk_b, wuv_b = w_uk.astype(BF16), w_uv.astype(BF16)
    wo0, wo1, w1 = w_out0.astype(BF16), w_out1.astype(BF16), w_in1.astype(BF16)
    row = lambda v: v.reshape(1, -1)
    tabq_p, tabk_p = _rope_tables(jnp.arange(t_p))
    tabq_s, tabk_s = _rope_tables(jnp.full((m_s,), past_len, jnp.int32))

    xp = x_prompt.reshape(m_p, D_MODEL)
    xs3 = x_sample.reshape(1, m_s, D_MODEL)
    xs = xs3.reshape(m_s, D_MODEL)

    proj = lambda x3, tq_, tk_, tm: _l0_proj(x3, tq_, tk_, wq0, wk0, row(g_cq), g_ckv.reshape(-1, 1), wuq,
                                             wuk_b.T, wuv_b.T, emat_t, tm)
    qa_p, gate_p, qn_p, qr_p, kvf_p, lat_p, kvb_p, kcat_p, vb_p = proj(x_prompt, tabq_p, tabk_p, tm_p)
    qa_s, gate_s, qn_s, qr_s, kvf_s, lat_s, _, _, _ = proj(xs3, tabq_s, tabk_s, tm_s)

    oa_p = _sb_attn(qa_p, kvb_p, jnp.concatenate([tri_blk, tri_blk], axis=0)).reshape(m_p, WIDTH)
    ob_p = _mla_attn(qn_p, qr_p, kcat_p, vb_p).reshape(m_p, WIDTH)
    h_p = _finish0(xp, oa_p, ob_p, gate_p.reshape(m_p, 2 * WIDTH), wo0, row(ln0_g), row(ln0_b), tm_p)

    lat_s_rows = lat_s[0].T
    oa_s = _sb_decode(page_table, _kv_pages(cache_a_kv), qa_s[0], tri_page).reshape(m_s, WIDTH)
    ob_s = _mla_decode(page_table, jnp.transpose(cache_b_latent, (0, 2, 1)), qn_s[0], qr_s[0], lat_s_rows,
                       wuk_b.T, wuv_b, gsel, n_chunks=2).reshape(m_s, WIDTH)
    h_s = _finish0(xs, oa_s, ob_s, gate_s[0], wo0, row(ln0_g), row(ln0_b), tm_s)

    h_p3 = h_p.reshape(bsz, t_p, D_MODEL)
    outs_p = _l1_proj_streams(h_p3, w1, tm_p)
    q_p, kvb1_p, gate1_p = outs_p[0:3], outs_p[3:6], outs_p[6]
    os_p, lses_p, c_new_p = [], [], []
    for g, c in enumerate((cache_c0_kv, cache_c1_kv, cache_c2_kv)):
        o, lse = _dil_attn(q_p[g], kvb1_p[g], g)
        os_p.append(o)
        lses_p.append(lse)
        win = min(c.shape[1], t_p)
        w_kv_t = w1[:, g * 3 * WIDTH + WIDTH:(g + 1) * 3 * WIDTH].T
        c_new_p.append(_kv_rows(_proj_t(h_p3, w_kv_t, win, min(win, 2 * MXU_TILE)), bsz, win))
    y_p = _finish1_streams(h_p3, os_p, lses_p, gate1_p, wo1, row(ln1_g), row(ln1_b), tm_p)

    outs_s = _l1_proj(h_s, w1, tm_s, F32)
    q_s, kvf1_s, gate1_s = outs_s[0:3], outs_s[3:6], outs_s[6]
    oc_s = _dil_decode(jnp.concatenate(q_s, axis=1), jnp.concatenate(kvf1_s, axis=1),
                       [_kv_pages(c) for c in (cache_c0_kv, cache_c1_kv, cache_c2_kv)]).reshape(m_s, WIDTH)
    y_s = _finish1(h_s, [oc_s], [], gate1_s, wo1, row(ln1_g), row(ln1_b), tm_s)

    kv5 = lambda a: a.reshape(db, t_s, 2, N_HEADS, HEAD_DIM)
    return (y_p.reshape(bsz, t_p, D_MODEL), y_s.reshape(db, t_s, D_MODEL),
            _kv_rows(kvf_p, bsz, t_p), kv5(kvf_s[0].T),
            jnp.transpose(lat_p, (0, 2, 1)), lat_s_rows.reshape(db, t_s, B_LATENT),
            c_new_p[0], kv5(kvf1_s[0]), c_new_p[1], kv5(kvf1_s[1]), c_new_p[2], kv5(kvf1_s[2]))
```
